```python
import math
import jax, jax.numpy as jnp
from jax import lax
import numpy as np

D_MODEL = 1024
BATCH = 4
SEQ = 8192
DEPTH = 1

HEAD_DIM = 64
SB_HEADS = 8
SB_WIDTH = SB_HEADS * HEAD_DIM
DIFF_HEADS = 4
DIFF_QK_WIDTH = DIFF_HEADS * 2 * HEAD_DIM
DIFF_V_DIM = 2 * HEAD_DIM
DIFF_WIDTH = DIFF_HEADS * DIFF_V_DIM
BLOCK_Q = 128
NORM_EPS = 1e-6
SPLIT_SIZES = (SB_WIDTH, SB_WIDTH, SB_WIDTH, SB_WIDTH,
               DIFF_QK_WIDTH, DIFF_QK_WIDTH, DIFF_WIDTH, DIFF_WIDTH,
               D_MODEL, D_MODEL)
IN_COLS = sum(SPLIT_SIZES)

kernel_name = "stickbreak_diffattn_gated_hybrid"


def rms_norm(x, g):
    x32 = x.astype(jnp.float32)
    y = x32 * lax.rsqrt(jnp.mean(x32 * x32, axis=-1, keepdims=True) + NORM_EPS)
    return (y * g.astype(jnp.float32)).astype(x.dtype)


def split_cols(proj):
    parts, off = [], 0
    for n in SPLIT_SIZES:
        parts.append(proj[..., off:off + n])
        off += n
    return parts


def alibi_slopes(n_heads):
    return jnp.asarray(np.array([2.0 ** (-8.0 * (h + 1) / n_heads) for h in range(n_heads)], dtype=np.float32))


def to_blocks(t):
    B, S = t.shape[:2]
    t = t.reshape((B, S // BLOCK_Q, BLOCK_Q) + t.shape[2:])
    return jnp.moveaxis(t, 1, 0)


def from_blocks(t):
    t = jnp.moveaxis(t, 0, 1)
    return t.reshape((t.shape[0], t.shape[1] * t.shape[2]) + t.shape[3:])


def stick_breaking_attention(q, k, v):
    S, D = q.shape[1], q.shape[-1]
    scale = D ** -0.5
    kpos = jnp.arange(S)
    starts = jnp.arange(S // BLOCK_Q) * BLOCK_Q

    def block(args):
        q_blk, start = args
        z = jnp.einsum('bqhd,bkhd->bhqk', q_blk, k, preferred_element_type=jnp.float32) * scale
        qpos = start + jnp.arange(BLOCK_Q)
        mask = kpos[None, :] < qpos[:, None]
        neg_log_1m_beta = jnp.where(mask, jax.nn.softplus(z), 0.0)
        suffix = lax.cumsum(neg_log_1m_beta, axis=3, reverse=True) - neg_log_1m_beta
        log_a = jax.nn.log_sigmoid(z) - suffix
        a = jnp.where(mask, jnp.exp(log_a), 0.0)
        return jnp.einsum('bhqk,bkhd->bqhd', a.astype(v.dtype), v)

    out = lax.map(block, (to_blocks(q), starts))
    return from_blocks(out)


def differential_attention(q, k, v, slopes, lam):
    S, D = q.shape[1], q.shape[-1]
    scale = D ** -0.5
    kpos = jnp.arange(S)
    starts = jnp.arange(S // BLOCK_Q) * BLOCK_Q

    def block(args):
        q_blk, start = args
        s = jnp.einsum('bqhmd,bkhmd->bhmqk', q_blk, k, preferred_element_type=jnp.float32) * scale
        qpos = start + jnp.arange(BLOCK_Q)
        dist = (qpos[:, None] - kpos[None, :]).astype(jnp.float32)
        bias = -slopes[:, None, None, None] * dist
        s = jnp.where(dist >= 0, s + bias, -jnp.inf)
        p = jax.nn.softmax(s, axis=-1)
        attn = p[:, :, 0] - lam * p[:, :, 1]
        return jnp.einsum('bhqk,bkhe->bqhe', attn.astype(v.dtype), v)

    out = lax.map(block, (to_blocks(q), starts))
    return from_blocks(out)


def setup_inputs(seed: int = 0) -> dict:
    key = jax.random.key(seed)
    ks = jax.random.split(key, 14)
    f32 = jnp.float32
    nrm = lambda k, shape, s: jax.random.normal(k, shape, f32) * s
    return {
        "x": nrm(ks[0], (BATCH, SEQ, D_MODEL), 1.0),
        "pre_norm_g": 1.0 + nrm(ks[1], (DEPTH, D_MODEL), 0.02),
        "w_in": nrm(ks[2], (DEPTH, D_MODEL, IN_COLS), D_MODEL ** -0.5),
        "b_gate": nrm(ks[3], (DEPTH, 2 * D_MODEL), 0.02),
        "lambda_q1": nrm(ks[4], (DEPTH, HEAD_DIM), 0.1),
        "lambda_k1": nrm(ks[5], (DEPTH, HEAD_DIM), 0.1),
        "lambda_q2": nrm(ks[6], (DEPTH, HEAD_DIM), 0.1),
        "lambda_k2": nrm(ks[7], (DEPTH, HEAD_DIM), 0.1),
        "subln_g": 1.0 + nrm(ks[8], (DEPTH, DIFF_V_DIM), 0.02),
        "w_o_sb": nrm(ks[9], (DEPTH, SB_WIDTH, D_MODEL), SB_WIDTH ** -0.5),
        "w_o_diff": nrm(ks[10], (DEPTH, DIFF_WIDTH, D_MODEL), DIFF_WIDTH ** -0.5),
        "w_out": nrm(ks[11], (DEPTH, D_MODEL, D_MODEL), D_MODEL ** -0.5),
        "post_norm_g": 1.0 + nrm(ks[12], (DEPTH, D_MODEL), 0.02),
    }


def reference(x, pre_norm_g, w_in, b_gate, lambda_q1, lambda_k1, lambda_q2, lambda_k2,
              subln_g, w_o_sb, w_o_diff, w_out, post_norm_g):
    B, S, _ = x.shape
    slopes = alibi_slopes(DIFF_HEADS)
    for layer in range(DEPTH):
        h = rms_norm(x, pre_norm_g[layer])
        proj = jnp.einsum('bsd,de->bse', h, w_in[layer])
        (sb_q, sb_k, sb_v, sb_z, df_q, df_k, df_v, df_z, g_sb, g_df) = split_cols(proj)

        sb_out = stick_breaking_attention(
            sb_q.reshape(B, S, SB_HEADS, HEAD_DIM),
            sb_k.reshape(B, S, SB_HEADS, HEAD_DIM),
            sb_v.reshape(B, S, SB_HEADS, HEAD_DIM)).reshape(B, S, SB_WIDTH)
        y_sb = jnp.einsum('bse,ed->bsd', sb_out * jax.nn.silu(sb_z), w_o_sb[layer])

        lambda_init = 0.8 - 0.6 * math.exp(-0.3 * layer)
        lam = (jnp.exp(jnp.sum(lambda_q1[layer] * lambda_k1[layer]).astype(jnp.float32))
               - jnp.exp(jnp.sum(lambda_q2[layer] * lambda_k2[layer]).astype(jnp.float32))
               + lambda_init)
        df_out = differential_attention(
            df_q.reshape(B, S, DIFF_HEADS, 2, HEAD_DIM),
            df_k.reshape(B, S, DIFF_HEADS, 2, HEAD_DIM),
            df_v.reshape(B, S, DIFF_HEADS, DIFF_V_DIM), slopes, lam)
        df_out = (rms_norm(df_out, subln_g[layer]) * (1.0 - lambda_init)).reshape(B, S, DIFF_WIDTH)
        y_df = jnp.einsum('bse,ed->bsd', df_out * jax.nn.silu(df_z), w_o_diff[layer])

        gates = jax.nn.sigmoid(jnp.concatenate([g_sb, g_df], axis=-1) + b_gate[layer])
        merged = gates[..., :D_MODEL] * y_sb + gates[..., D_MODEL:] * y_df
        out = jnp.einsum('bsd,de->bse', merged, w_out[layer])
        x = x + rms_norm(out, post_norm_g[layer])
    return x
```

```python
import functools
import math

import jax
import jax.numpy as jnp
from jax import lax
from jax.experimental import pallas as pl
from jax.experimental.pallas import tpu as pltpu

F32 = jnp.float32
BF16 = jnp.bfloat16

HEAD_DIM = 64
SB_HEADS = 8
DIFF_HEADS = 4
SB_WIDTH = SB_HEADS * HEAD_DIM
DIFF_WIDTH = DIFF_HEADS * 2 * HEAD_DIM
NORM_EPS = 1e-6
LANES = 128
LOG2E = 1.4426950408889634
QK_SCALE = HEAD_DIM ** -0.5
NEG_BIG = -1e30
VMEM_LIMIT_BYTES = 56 * 1024 * 1024

QKV_COLS = 3 * SB_WIDTH + 3 * DIFF_WIDTH
SB_Q_BLK, SB_K_BLK, SB_V_BLK = 0, 4, 8
DF_Q_BLK, DF_K_BLK, DF_V_BLK = 12, 16, 20


def _rms_norm(x32, g32):
    return x32 * lax.rsqrt(jnp.mean(x32 * x32, axis=-1, keepdims=True) + NORM_EPS) * g32


def _split_halves(q):
    q32 = q.astype(F32)
    lane = lax.broadcasted_iota(jnp.int32, q.shape, 1)
    return (jnp.where(lane < HEAD_DIM, q32, 0.0).astype(BF16),
            jnp.where(lane >= HEAD_DIM, q32, 0.0).astype(BF16))


def _qkv_kernel(x_ref, g_ref, w_ref, cs_ref, o_ref):
    h = _rms_norm(x_ref[...], g_ref[...]).astype(BF16)
    p = jnp.dot(h, w_ref[...], preferred_element_type=F32)
    o_ref[...] = (p * cs_ref[...]).astype(BF16)


def _qkv_proj(x2d, g, w_qkv, col_scale, tm):
    n, d = x2d.shape
    return pl.pallas_call(
        _qkv_kernel,
        grid=(n // tm,),
        in_specs=[
            pl.BlockSpec((tm, d), lambda i: (i, 0)),
            pl.BlockSpec((1, d), lambda i: (0, 0)),
            pl.BlockSpec((d, QKV_COLS), lambda i: (0, 0)),
            pl.BlockSpec((1, QKV_COLS), lambda i: (0, 0)),
        ],
        out_specs=pl.BlockSpec((tm, QKV_COLS), lambda i: (i, 0)),
        out_shape=jax.ShapeDtypeStruct((n, QKV_COLS), BF16),
        compiler_params=pltpu.CompilerParams(
            dimension_semantics=("arbitrary",), vmem_limit_bytes=VMEM_LIMIT_BYTES),
        name="qkv_proj",
    )(x2d, g, w_qkv, col_scale)


def _sb_kernel(q_ref, k_ref, v_ref, u_ref, o_ref, acc_ref, carry_ref, *, t):
    qi = pl.program_id(2)
    q_heads = _split_halves(q_ref[0])
    lane = lax.broadcasted_iota(jnp.int32, (t, LANES), 1)
    u = u_ref[...]

    acc_ref[...] = jnp.zeros_like(acc_ref)
    carry_ref[...] = jnp.zeros_like(carry_ref)

    def tile(kt, vt, mask):
        for h in range(2):
            z = lax.dot_general(q_heads[h], kt, (((1,), (1,)), ((), ())),
                                preferred_element_type=F32)
            e = jnp.exp2(-jnp.abs(z))
            sp = jnp.maximum(z, 0.0) + jnp.log(1.0 + e) * LOG2E
            if mask is not None:
                sp = jnp.where(mask, sp, 0.0)
            hi = sp.astype(BF16)
            lo = (sp - hi.astype(F32)).astype(BF16)
            cum = (jnp.dot(hi, u, preferred_element_type=F32)
                   + jnp.dot(lo, u, preferred_element_type=F32))
            c = carry_ref[h]
            a = jnp.exp2(z - cum - jnp.concatenate([c] * (t // LANES), axis=1))
            if mask is not None:
                a = jnp.where(mask, a, 0.0)
            acc_ref[h] += jnp.dot(a.astype(BF16), vt, preferred_element_type=F32)
            carry_ref[h] = c + jnp.broadcast_to(cum[:, 0:1], (t, LANES))

    row = lax.broadcasted_iota(jnp.int32, (t, t), 0)
    col = lax.broadcasted_iota(jnp.int32, (t, t), 1)
    d0 = pl.multiple_of(qi * t, t)
    tile(k_ref[0, pl.ds(d0, t), :], v_ref[0, pl.ds(d0, t), :], col < row)

    def body(it, _):
        k0 = pl.multiple_of((qi - 1 - it) * t, t)
        tile(k_ref[0, pl.ds(k0, t), :], v_ref[0, pl.ds(k0, t), :], None)
        return 0

    lax.fori_loop(0, qi, body, 0)
    o_ref[0] = jnp.where(lane < HEAD_DIM, acc_ref[0], acc_ref[1])


def _sb_attn(qkv, u, t):
    b, s, _ = qkv.shape
    pairs = SB_HEADS // 2
    return pl.pallas_call(
        functools.partial(_sb_kernel, t=t),
        grid=(b, pairs, s // t),
        in_specs=[
            pl.BlockSpec((1, t, LANES), lambda bi, p, qi: (bi, qi, SB_Q_BLK + p)),
            pl.BlockSpec((1, s, LANES), lambda bi, p, qi: (bi, 0, SB_K_BLK + p)),
            pl.BlockSpec((1, s, LANES), lambda bi, p, qi: (bi, 0, SB_V_BLK + p)),
            pl.BlockSpec((t, t), lambda bi, p, qi: (0, 0)),
        ],
        out_specs=pl.BlockSpec((1, t, LANES), lambda bi, p, qi: (bi, qi, p)),
        out_shape=jax.ShapeDtypeStruct((b, s, SB_WIDTH), F32),
        scratch_shapes=[pltpu.VMEM((2, t, LANES), F32), pltpu.VMEM((2, t, LANES), F32)],
        compiler_params=pltpu.CompilerParams(
            dimension_semantics=("arbitrary", "arbitrary", "arbitrary"),
            vmem_limit_bytes=VMEM_LIMIT_BYTES),
        name="sb_attn",
    )(qkv, qkv, qkv, u)


def _diff_kernel(q_ref, k_ref, v_ref, lam_ref, slope_ref, g_ref, o_ref, acc_ref, m_ref, *, t,
                 out_scale):
    hd = pl.program_id(1)
    qi = pl.program_id(2)
    q_maps = _split_halves(q_ref[0])
    slope2 = slope_ref[hd]
    key_iota = lax.broadcasted_iota(jnp.int32, (1, t), 1)
    ones = jnp.ones((t, LANES), BF16)
    reps = t // LANES

    acc_ref[...] = jnp.zeros_like(acc_ref)
    m_ref[...] = jnp.full_like(m_ref, NEG_BIG)

    def tile(kt, vt, k_off, mask):
        bias = (key_iota + k_off).astype(F32) * slope2
        v_ext = jnp.concatenate([vt, ones], axis=1)
        for mp in range(2):
            s2 = lax.dot_general(q_maps[mp], kt, (((1,), (1,)), ((), ())),
                                 preferred_element_type=F32) + bias
            if mask is not None:
                s2 = jnp.where(mask, s2, NEG_BIG)
            m_old = m_ref[mp]
            m_new = jnp.maximum(m_old, jnp.max(s2, axis=-1, keepdims=True))
            p = jnp.exp2(s2 - jnp.concatenate([m_new] * reps, axis=1))
            alpha = jnp.exp2(m_old - m_new)
            acc_ref[mp] = (acc_ref[mp] * jnp.concatenate([alpha, alpha], axis=1)
                           + jnp.dot(p.astype(BF16), v_ext, preferred_element_type=F32))
            m_ref[mp] = m_new

    row = lax.broadcasted_iota(jnp.int32, (t, t), 0)
    col = lax.broadcasted_iota(jnp.int32, (t, t), 1)
    d0 = pl.multiple_of(qi * t, t)
    tile(k_ref[0, pl.ds(d0, t), :], v_ref[0, pl.ds(d0, t), :], jnp.int32(0), col <= row)

    def body(it, _):
        k0 = pl.multiple_of((qi - 1 - it) * t, t)
        tile(k_ref[0, pl.ds(k0, t), :], v_ref[0, pl.ds(k0, t), :], -(it + 1) * t, None)
        return 0

    lax.fori_loop(0, qi, body, 0)

    a1 = acc_ref[0]
    a2 = acc_ref[1]
    o = a1[:, :LANES] / a1[:, LANES:] - lam_ref[0, 0] * (a2[:, :LANES] / a2[:, LANES:])
    o_ref[0] = _rms_norm(o, g_ref[...]) * out_scale


def _diff_attn(qkv, lam, subln_g, t, out_scale):
    b, s, _ = qkv.shape
    slopes2 = jnp.asarray([2.0 ** (-8.0 * (h + 1) / DIFF_HEADS) * LOG2E for h in range(DIFF_HEADS)], F32)
    return pl.pallas_call(
        functools.partial(_diff_kernel, t=t, out_scale=out_scale),
        grid=(b, DIFF_HEADS, s // t),
        in_specs=[
            pl.BlockSpec((1, t, LANES), lambda bi, h, qi: (bi, qi, DF_Q_BLK + h)),
            pl.BlockSpec((1, s, LANES), lambda bi, h, qi: (bi, 0, DF_K_BLK + h)),
            pl.BlockSpec((1, s, LANES), lambda bi, h, qi: (bi, 0, DF_V_BLK + h)),
            pl.BlockSpec(memory_space=pltpu.SMEM),
            pl.BlockSpec(memory_space=pltpu.SMEM),
            pl.BlockSpec((1, LANES), lambda bi, h, qi: (0, 0)),
        ],
        out_specs=pl.BlockSpec((1, t, LANES), lambda bi, h, qi: (bi, qi, h)),
        out_shape=jax.ShapeDtypeStruct((b, s, DIFF_WIDTH), F32),
        scratch_shapes=[pltpu.VMEM((2, t, 2 * LANES), F32), pltpu.VMEM((2, t, LANES), F32)],
        compiler_params=pltpu.CompilerParams(
            dimension_semantics=("arbitrary", "arbitrary", "arbitrary"),
            vmem_limit_bytes=VMEM_LIMIT_BYTES),
        name="diff_attn",
    )(qkv, qkv, qkv, lam, slopes2, subln_g)


def _lambda_kernel(q1_ref, k1_ref, q2_ref, k2_ref, o_ref, *, lambda_init):
    s1 = jnp.sum(q1_ref[...] * k1_ref[...], axis=-1, keepdims=True)
    s2 = jnp.sum(q2_ref[...] * k2_ref[...], axis=-1, keepdims=True)
    o_ref[...] = jnp.exp(s1) - jnp.exp(s2) + lambda_init


def _lambda(q1, k1, q2, k2, lambda_init):
    return pl.pallas_call(
        functools.partial(_lambda_kernel, lambda_init=lambda_init),
        out_shape=jax.ShapeDtypeStruct((1, 1), F32),
        name="diff_lambda",
    )(q1, k1, q2, k2)


def _out_kernel(x_ref, sb_ref, df_ref, g_pre_ref, w_zg_ref, b_ref, w_sb_ref, w_df_ref, w_out_ref,
                g_post_ref, o_ref):
    x = x_ref[...]
    h = _rms_norm(x, g_pre_ref[...]).astype(BF16)
    zg = jnp.dot(h, w_zg_ref[...], preferred_element_type=F32)
    d = x.shape[-1]
    sb_z = zg[:, :SB_WIDTH]
    df_z = zg[:, SB_WIDTH:SB_WIDTH + DIFF_WIDTH]
    gates = jax.nn.sigmoid(zg[:, SB_WIDTH + DIFF_WIDTH:] + b_ref[...])
    y_sb = jnp.dot((sb_ref[...] * (sb_z * jax.nn.sigmoid(sb_z))).astype(BF16), w_sb_ref[...],
                   preferred_element_type=F32)
    y_df = jnp.dot((df_ref[...] * (df_z * jax.nn.sigmoid(df_z))).astype(BF16), w_df_ref[...],
                   preferred_element_type=F32)
    merged = gates[:, :d] * y_sb + gates[:, d:] * y_df
    out = jnp.dot(merged.astype(BF16), w_out_ref[...], preferred_element_type=F32)
    o_ref[...] = x + _rms_norm(out, g_post_ref[...])


def _out_proj(x2d, sb, df, g_pre, w_zg, b_gate, w_sb, w_df, w_out, g_post, tm):
    n, d = x2d.shape
    zg_cols = w_zg.shape[1]
    full = lambda shape: pl.BlockSpec(shape, lambda i: (0, 0))
    return pl.pallas_call(
        _out_kernel,
        grid=(n // tm,),
        in_specs=[
            pl.BlockSpec((tm, d), lambda i: (i, 0)),
            pl.BlockSpec((tm, SB_WIDTH), lambda i: (i, 0)),
            pl.BlockSpec((tm, DIFF_WIDTH), lambda i: (i, 0)),
            full((1, d)),
            full((d, zg_cols)),
            full((1, 2 * d)),
            full((SB_WIDTH, d)),
            full((DIFF_WIDTH, d)),
            full((d, d)),
            full((1, d)),
        ],
        out_specs=pl.BlockSpec((tm, d), lambda i: (i, 0)),
        out_shape=jax.ShapeDtypeStruct((n, d), F32),
        compiler_params=pltpu.CompilerParams(
            dimension_semantics=("arbitrary",), vmem_limit_bytes=VMEM_LIMIT_BYTES),
        name="out_proj",
    )(x2d, sb, df, g_pre, w_zg, b_gate, w_sb, w_df, w_out, g_post)


def _split_w_in(w):
    sizes = (SB_WIDTH,) * 4 + (DIFF_WIDTH,) * 4 + (w.shape[0],) * 2
    parts, off = [], 0
    for n in sizes:
        parts.append(w[:, off:off + n])
        off += n
    return parts


def kernel(x, pre_norm_g, w_in, b_gate, lambda_q1, lambda_k1, lambda_q2, lambda_k2, subln_g, w_o_sb,
           w_o_diff, w_out, post_norm_g):
    b, s, d = x.shape
    depth = w_in.shape[0]
    t_attn = min(256, s)
    tm = min(256, b * s)
    u = (lax.broadcasted_iota(jnp.int32, (t_attn, t_attn), 0)
         >= lax.broadcasted_iota(jnp.int32, (t_attn, t_attn), 1)).astype(BF16)
    q_scale = jnp.full((1, SB_WIDTH), QK_SCALE * LOG2E, F32)
    one = jnp.ones((1, SB_WIDTH), F32)
    col_scale = jnp.concatenate([q_scale, one, one, q_scale, one, one], axis=1)
    for layer in range(depth):
        sb_q, sb_k, sb_v, sb_z, df_q, df_k, df_v, df_z, g_sb, g_df = _split_w_in(w_in[layer])
        w_qkv = jnp.concatenate([sb_q, sb_k, sb_v, df_q, df_k, df_v], axis=1).astype(BF16)
        w_zg = jnp.concatenate([sb_z, df_z, g_sb, g_df], axis=1).astype(BF16)
        lambda_init = 0.8 - 0.6 * math.exp(-0.3 * layer)
        x2d = x.reshape(b * s, d)
        g_pre = pre_norm_g[layer].reshape(1, d)

        qkv = _qkv_proj(x2d, g_pre, w_qkv, col_scale, tm).reshape(b, s, QKV_COLS)
        sb_out = _sb_attn(qkv, u, t_attn)
        lam = _lambda(lambda_q1[layer][None], lambda_k1[layer][None], lambda_q2[layer][None],
                      lambda_k2[layer][None], lambda_init)
        df_out = _diff_attn(qkv, lam, subln_g[layer].reshape(1, LANES), t_attn, 1.0 - lambda_init)
        x = _out_proj(x2d, sb_out.reshape(b * s, SB_WIDTH), df_out.reshape(b * s, DIFF_WIDTH), g_pre,
                      w_zg, b_gate[layer].reshape(1, 2 * d), w_o_sb[layer].astype(BF16),
                      w_o_diff[layer].astype(BF16), w_out[layer].astype(BF16),
                      post_norm_g[layer].reshape(1, d), tm).reshape(b, s, d)
    return x
```

```python
import functools
import math

import jax
import jax.numpy as jnp
from jax import lax
from jax.experimental import pallas as pl
from jax.experimental.pallas import tpu as pltpu

F32 = jnp.float32
BF16 = jnp.bfloat16

HEAD_DIM = 64
SB_HEADS = 8
DIFF_HEADS = 4
SB_WIDTH = SB_HEADS * HEAD_DIM
DIFF_WIDTH = DIFF_HEADS * 2 * HEAD_DIM
NORM_EPS = 1e-6
LANES = 128
MXU_DIM = 256
LOG2E = 1.4426950408889634
QK_SCALE = HEAD_DIM ** -0.5
NEG_BIG = -1e30
VMEM_LIMIT_BYTES = 56 * 1024 * 1024

SUM_W = MXU_DIM
ATTN_T = 2 * SUM_W

QKV_COLS = 3 * SB_WIDTH + 3 * DIFF_WIDTH
SB_Q_BLK, SB_K_BLK, SB_V_BLK = 0, 4, 8
DF_Q_BLK, DF_K_BLK, DF_V_BLK = 12, 16, 20

_NT = (((1,), (1,)), ((), ()))


def _rms_norm(x32, g32):
    return x32 * lax.rsqrt(jnp.mean(x32 * x32, axis=-1, keepdims=True) + NORM_EPS) * g32


def _stack_halves(q):
    q32 = q.astype(F32)
    lane = lax.broadcasted_iota(jnp.int32, q.shape, 1)
    return jnp.concatenate([jnp.where(lane < HEAD_DIM, q32, 0.0).astype(BF16),
                            jnp.where(lane >= HEAD_DIM, q32, 0.0).astype(BF16)], axis=0)


def _qkv_kernel(x_ref, g_ref, w_ref, cs_ref, o_ref):
    h = _rms_norm(x_ref[...], g_ref[...]).astype(BF16)
    p = jnp.dot(h, w_ref[...], preferred_element_type=F32)
    o_ref[...] = (p * cs_ref[...]).astype(BF16)


def _qkv_proj(x2d, g, w_qkv, col_scale, tm):
    n, d = x2d.shape
    return pl.pallas_call(
        _qkv_kernel,
        grid=(n // tm,),
        in_specs=[
            pl.BlockSpec((tm, d), lambda i: (i, 0)),
            pl.BlockSpec((1, d), lambda i: (0, 0)),
            pl.BlockSpec((d, QKV_COLS), lambda i: (0, 0)),
            pl.BlockSpec((1, QKV_COLS), lambda i: (0, 0)),
        ],
        out_specs=pl.BlockSpec((tm, QKV_COLS), lambda i: (i, 0)),
        out_shape=jax.ShapeDtypeStruct((n, QKV_COLS), BF16),
        compiler_params=pltpu.CompilerParams(
            dimension_semantics=("arbitrary",), vmem_limit_bytes=VMEM_LIMIT_BYTES),
        name="qkv_proj",
    )(x2d, g, w_qkv, col_scale)


def _sb_kernel(q_ref, k_ref, v_ref, u_ref, o_ref, acc_ref, carry_ref, z_ref, hilo_ref, arg_ref, *,
               t, w):
    qi = pl.program_id(2)
    rows = 2 * t
    nsub = t // w
    q2 = _stack_halves(q_ref[0])

    acc_ref[...] = jnp.zeros_like(acc_ref)
    carry_ref[...] = jnp.zeros_like(carry_ref)

    def qk(kb):
        kt = k_ref[0, pl.ds(pl.multiple_of(kb * t, t), t), :]
        return lax.dot_general(q2, kt, _NT, preferred_element_type=F32)

    def scores(z, valid):
        sp = jnp.maximum(z, 0.0) + jnp.log(1.0 + jnp.exp2(-jnp.abs(z))) * LOG2E
        if valid is not None:
            sp = jnp.where(valid, sp, 0.0)
            z = jnp.where(valid, z, NEG_BIG)
        hi = sp.astype(BF16)
        lo = (sp - hi.astype(F32)).astype(BF16)
        for sb in range(nsub):
            hilo_ref[:, 2 * sb * w:(2 * sb + 1) * w] = hi[:, sb * w:(sb + 1) * w]
            hilo_ref[:, (2 * sb + 1) * w:(2 * sb + 2) * w] = lo[:, sb * w:(sb + 1) * w]
        z_ref[...] = z

    def sums():
        c = carry_ref[...]
        for sb in reversed(range(nsub)):
            cum = jnp.dot(hilo_ref[:, 2 * sb * w:(2 * sb + 2) * w], u_ref[...],
                          preferred_element_type=F32)
            arg_ref[:, sb * w:(sb + 1) * w] = (z_ref[:, sb * w:(sb + 1) * w] - cum
                                               - jnp.concatenate([c] * (w // LANES), axis=1))
            c = c + jnp.broadcast_to(cum[:, 0:1], (rows, LANES))
        carry_ref[...] = c

    def values(kb):
        vt = v_ref[0, pl.ds(pl.multiple_of(kb * t, t), t), :]
        acc_ref[...] += jnp.dot(jnp.exp2(arg_ref[...]).astype(BF16), vt, preferred_element_type=F32)

    row = lax.broadcasted_iota(jnp.int32, (rows, t), 0) & (t - 1)
    col = lax.broadcasted_iota(jnp.int32, (rows, t), 1)
    scores(qk(qi), col < row)
    sums()
    scores(qk(jnp.maximum(qi - 1, 0)), jnp.broadcast_to(qi > 0, (rows, t)))

    def body(it, _):
        values(qi - it)
        sums()
        scores(qk(qi - 2 - it), None)
        return 0

    lax.fori_loop(0, qi - 1, body, 0)
    values(jnp.minimum(qi, 1))
    sums()
    values(0)
    lane = lax.broadcasted_iota(jnp.int32, (t, LANES), 1)
    o_ref[0] = jnp.where(lane < HEAD_DIM, acc_ref[:t], acc_ref[t:])


def _sb_attn(qkv, u2, t, w):
    b, s, _ = qkv.shape
    rows = 2 * t
    return pl.pallas_call(
        functools.partial(_sb_kernel, t=t, w=w),
        grid=(b, SB_HEADS // 2, s // t),
        in_specs=[
            pl.BlockSpec((1, t, LANES), lambda bi, p, qi: (bi, qi, SB_Q_BLK + p)),
            pl.BlockSpec((1, s, LANES), lambda bi, p, qi: (bi, 0, SB_K_BLK + p)),
            pl.BlockSpec((1, s, LANES), lambda bi, p, qi: (bi, 0, SB_V_BLK + p)),
            pl.BlockSpec((2 * w, w), lambda bi, p, qi: (0, 0)),
        ],
        out_specs=pl.BlockSpec((1, t, LANES), lambda bi, p, qi: (bi, qi, p)),
        out_shape=jax.ShapeDtypeStruct((b, s, SB_WIDTH), F32),
        scratch_shapes=[
            pltpu.VMEM((rows, LANES), F32),
            pltpu.VMEM((rows, LANES), F32),
            pltpu.VMEM((rows, t), F32),
            pltpu.VMEM((rows, 2 * t), BF16),
            pltpu.VMEM((rows, t), F32),
        ],
        compiler_params=pltpu.CompilerParams(
            dimension_semantics=("arbitrary", "arbitrary", "arbitrary"),
            vmem_limit_bytes=VMEM_LIMIT_BYTES),
        name="sb_attn",
    )(qkv, qkv, qkv, u2)


def _diff_kernel(q_ref, k_ref, v_ref, lam_ref, slope_ref, g_ref, o_ref, acc_ref, m_ref, alpha_ref,
                 s_ref, *, t, out_scale):
    hd = pl.program_id(1)
    qi = pl.program_id(2)
    rows = 2 * t
    reps = t // LANES
    q2 = _stack_halves(q_ref[0])
    slope2 = slope_ref[hd]
    key_iota = lax.broadcasted_iota(jnp.int32, (1, t), 1)
    ones = jnp.ones((t, LANES), BF16)

    acc_ref[...] = jnp.zeros_like(acc_ref)
    m_ref[...] = jnp.full_like(m_ref, NEG_BIG)

    def qk(kb):
        kt = k_ref[0, pl.ds(pl.multiple_of(kb * t, t), t), :]
        return lax.dot_general(q2, kt, _NT, preferred_element_type=F32)

    def scores(s2, kb, valid):
        bias = (key_iota + (kb - qi) * t).astype(F32) * slope2
        s2 = s2 + bias
        if valid is not None:
            s2 = jnp.where(valid, s2, NEG_BIG)
        m_old = m_ref[...]
        m_new = jnp.maximum(m_old, jnp.max(s2, axis=-1, keepdims=True))
        s_ref[...] = s2 - jnp.concatenate([m_new] * reps, axis=1)
        alpha_ref[...] = jnp.exp2(m_old - m_new)
        m_ref[...] = m_new

    def values(kb):
        vt = v_ref[0, pl.ds(pl.multiple_of(kb * t, t), t), :]
        alpha = alpha_ref[...]
        acc_ref[...] = (acc_ref[...] * jnp.concatenate([alpha, alpha], axis=1)
                        + jnp.dot(jnp.exp2(s_ref[...]).astype(BF16),
                                  jnp.concatenate([vt, ones], axis=1), preferred_element_type=F32))

    row = lax.broadcasted_iota(jnp.int32, (rows, t), 0) & (t - 1)
    col = lax.broadcasted_iota(jnp.int32, (rows, t), 1)
    scores(qk(qi), qi, col <= row)

    def body(it, _):
        s2 = qk(qi - 1 - it)
        values(qi - it)
        scores(s2, qi - 1 - it, None)
        return 0

    lax.fori_loop(0, qi, body, 0)
    values(0)

    a1 = acc_ref[:t]
    a2 = acc_ref[t:]
    o = a1[:, :LANES] / a1[:, LANES:] - lam_ref[0, 0] * (a2[:, :LANES] / a2[:, LANES:])
    o_ref[0] = _rms_norm(o, g_ref[...]) * out_scale


def _diff_attn(qkv, lam, subln_g, t, out_scale):
    b, s, _ = qkv.shape
    rows = 2 * t
    slopes2 = jnp.asarray([2.0 ** (-8.0 * (h + 1) / DIFF_HEADS) * LOG2E for h in range(DIFF_HEADS)], F32)
    return pl.pallas_call(
        functools.partial(_diff_kernel, t=t, out_scale=out_scale),
        grid=(b, DIFF_HEADS, s // t),
        in_specs=[
            pl.BlockSpec((1, t, LANES), lambda bi, h, qi: (bi, qi, DF_Q_BLK + h)),
            pl.BlockSpec((1, s, LANES), lambda bi, h, qi: (bi, 0, DF_K_BLK + h)),
            pl.BlockSpec((1, s, LANES), lambda bi, h, qi: (bi, 0, DF_V_BLK + h)),
            pl.BlockSpec(memory_space=pltpu.SMEM),
            pl.BlockSpec(memory_space=pltpu.SMEM),
            pl.BlockSpec((1, LANES), lambda bi, h, qi: (0, 0)),
        ],
        out_specs=pl.BlockSpec((1, t, LANES), lambda bi, h, qi: (bi, qi, h)),
        out_shape=jax.ShapeDtypeStruct((b, s, DIFF_WIDTH), F32),
        scratch_shapes=[
            pltpu.VMEM((rows, 2 * LANES), F32),
            pltpu.VMEM((rows, LANES), F32),
            pltpu.VMEM((rows, LANES), F32),
            pltpu.VMEM((rows, t), F32),
        ],
        compiler_params=pltpu.CompilerParams(
            dimension_semantics=("arbitrary", "arbitrary", "arbitrary"),
            vmem_limit_bytes=VMEM_LIMIT_BYTES),
        name="diff_attn",
    )(qkv, qkv, qkv, lam, slopes2, subln_g)


def _lambda_kernel(q1_ref, k1_ref, q2_ref, k2_ref, o_ref, *, lambda_init):
    s1 = jnp.sum(q1_ref[...] * k1_ref[...], axis=-1, keepdims=True)
    s2 = jnp.sum(q2_ref[...] * k2_ref[...], axis=-1, keepdims=True)
    o_ref[...] = jnp.exp(s1) - jnp.exp(s2) + lambda_init


def _lambda(q1, k1, q2, k2, lambda_init):
    return pl.pallas_call(
        functools.partial(_lambda_kernel, lambda_init=lambda_init),
        out_shape=jax.ShapeDtypeStruct((1, 1), F32),
        name="diff_lambda",
    )(q1, k1, q2, k2)


def _out_kernel(x_ref, sb_ref, df_ref, g_pre_ref, w_zg_ref, b_ref, w_sb_ref, w_df_ref, w_out_ref,
                g_post_ref, o_ref):
    x = x_ref[...]
    h = _rms_norm(x, g_pre_ref[...]).astype(BF16)
    zg = jnp.dot(h, w_zg_ref[...], preferred_element_type=F32)
    d = x.shape[-1]
    sb_z = zg[:, :SB_WIDTH]
    df_z = zg[:, SB_WIDTH:SB_WIDTH + DIFF_WIDTH]
    gates = jax.nn.sigmoid(zg[:, SB_WIDTH + DIFF_WIDTH:] + b_ref[...])
    y_sb = jnp.dot((sb_ref[...] * (sb_z * jax.nn.sigmoid(sb_z))).astype(BF16), w_sb_ref[...],
                   preferred_element_type=F32)
    y_df = jnp.dot((df_ref[...] * (df_z * jax.nn.sigmoid(df_z))).astype(BF16), w_df_ref[...],
                   preferred_element_type=F32)
    merged = gates[:, :d] * y_sb + gates[:, d:] * y_df
    out = jnp.dot(merged.astype(BF16), w_out_ref[...], preferred_element_type=F32)
    o_ref[...] = x + _rms_norm(out, g_post_ref[...])


def _out_proj(x2d, sb, df, g_pre, w_zg, b_gate, w_sb, w_df, w_out, g_post, tm):
    n, d = x2d.shape
    zg_cols = w_zg.shape[1]
    full = lambda shape: pl.BlockSpec(shape, lambda i: (0, 0))
    return pl.pallas_call(
        _out_kernel,
        grid=(n // tm,),
        in_specs=[
            pl.BlockSpec((tm, d), lambda i: (i, 0)),
            pl.BlockSpec((tm, SB_WIDTH), lambda i: (i, 0)),
            pl.BlockSpec((tm, DIFF_WIDTH), lambda i: (i, 0)),
            full((1, d)),
            full((d, zg_cols)),
            full((1, 2 * d)),
            full((SB_WIDTH, d)),
            full((DIFF_WIDTH, d)),
            full((d, d)),
            full((1, d)),
        ],
        out_specs=pl.BlockSpec((tm, d), lambda i: (i, 0)),
        out_shape=jax.ShapeDtypeStruct((n, d), F32),
        compiler_params=pltpu.CompilerParams(
            dimension_semantics=("arbitrary",), vmem_limit_bytes=VMEM_LIMIT_BYTES),
        name="out_proj",
    )(x2d, sb, df, g_pre, w_zg, b_gate, w_sb, w_df, w_out, g_post)


def _split_w_in(w):
    sizes = (SB_WIDTH,) * 4 + (DIFF_WIDTH,) * 4 + (w.shape[0],) * 2
    parts, off = [], 0
    for n in sizes:
        parts.append(w[:, off:off + n])
        off += n
    return parts


def kernel(x, pre_norm_g, w_in, b_gate, lambda_q1, lambda_k1, lambda_q2, lambda_k2, subln_g, w_o_sb,
           w_o_diff, w_out, post_norm_g):
    b, s, d = x.shape
    depth = w_in.shape[0]
    t, w = ATTN_T, SUM_W
    assert s % t == 0 and s >= 2 * t and t % w == 0
    tm = min(256, b * s)
    u = (lax.broadcasted_iota(jnp.int32, (w, w), 0)
         >= lax.broadcasted_iota(jnp.int32, (w, w), 1)).astype(BF16)
    u2 = jnp.concatenate([u, u], axis=0)
    q_scale = jnp.full((1, SB_WIDTH), QK_SCALE * LOG2E, F32)
    one = jnp.ones((1, SB_WIDTH), F32)
    col_scale = jnp.concatenate([q_scale, one, one, q_scale, one, one], axis=1)
    for layer in range(depth):
        sb_q, sb_k, sb_v, sb_z, df_q, df_k, df_v, df_z, g_sb, g_df = _split_w_in(w_in[layer])
        w_qkv = jnp.concatenate([sb_q, sb_k, sb_v, df_q, df_k, df_v], axis=1).astype(BF16)
        w_zg = jnp.concatenate([sb_z, df_z, g_sb, g_df], axis=1).astype(BF16)
        lambda_init = 0.8 - 0.6 * math.exp(-0.3 * layer)
        x2d = x.reshape(b * s, d)
        g_pre = pre_norm_g[layer].reshape(1, d)

        qkv = _qkv_proj(x2d, g_pre, w_qkv, col_scale, tm).reshape(b, s, QKV_COLS)
        sb_out = _sb_attn(qkv, u2, t, w)
        lam = _lambda(lambda_q1[layer][None], lambda_k1[layer][None], lambda_q2[layer][None],
                      lambda_k2[layer][None], lambda_init)
        df_out = _diff_attn(qkv, lam, subln_g[layer].reshape(1, LANES), t, 1.0 - lambda_init)
        x = _out_proj(x2d, sb_out.reshape(b * s, SB_WIDTH), df_out.reshape(b * s, DIFF_WIDTH), g_pre,
                      w_zg, b_gate[layer].reshape(1, 2 * d), w_o_sb[layer].astype(BF16),
                      w_o_diff[layer].astype(BF16), w_out[layer].astype(BF16),
                      post_norm_g[layer].reshape(1, d), tm).reshape(b, s, d)
    return x
```

```python
import functools
import math

import jax
import jax.numpy as jnp
from jax import lax
from jax.experimental import pallas as pl
from jax.experimental.pallas import tpu as pltpu

F32 = jnp.float32
BF16 = jnp.bfloat16

HEAD_DIM = 64
SB_HEADS = 8
DIFF_HEADS = 4
SB_WIDTH = SB_HEADS * HEAD_DIM
DIFF_WIDTH = DIFF_HEADS * 2 * HEAD_DIM
NORM_EPS = 1e-6
LANES = 128
MXU_DIM = 256
LOG2E = 1.4426950408889634
QK_SCALE = HEAD_DIM ** -0.5
NEG_BIG = -1e30
SOFTPLUS_CLAMP = 100.0
ZERO_WEIGHT_CARRY = 152.0
VMEM_LIMIT_BYTES = 56 * 1024 * 1024

SUM_W = MXU_DIM
ATTN_T = 2 * SUM_W

QKV_COLS = 3 * SB_WIDTH + 3 * DIFF_WIDTH
SB_Q_BLK, SB_K_BLK, SB_V_BLK = 0, 4, 8
DF_Q_BLK, DF_K_BLK, DF_V_BLK = 12, 16, 20

_NT = (((1,), (1,)), ((), ()))


def _rms_norm(x32, g32):
    return x32 * lax.rsqrt(jnp.mean(x32 * x32, axis=-1, keepdims=True) + NORM_EPS) * g32


def _stack_halves(q):
    q32 = q.astype(F32)
    lane = lax.broadcasted_iota(jnp.int32, q.shape, 1)
    return jnp.concatenate([jnp.where(lane < HEAD_DIM, q32, 0.0).astype(BF16),
                            jnp.where(lane >= HEAD_DIM, q32, 0.0).astype(BF16)], axis=0)


def _qkv_kernel(x_ref, g_ref, w_ref, cs_ref, o_ref):
    h = _rms_norm(x_ref[...], g_ref[...]).astype(BF16)
    p = jnp.dot(h, w_ref[...], preferred_element_type=F32)
    o_ref[...] = (p * cs_ref[...]).astype(BF16)


def _qkv_proj(x2d, g, w_qkv, col_scale, tm):
    n, d = x2d.shape
    return pl.pallas_call(
        _qkv_kernel,
        grid=(n // tm,),
        in_specs=[
            pl.BlockSpec((tm, d), lambda i: (i, 0)),
            pl.BlockSpec((1, d), lambda i: (0, 0)),
            pl.BlockSpec((d, QKV_COLS), lambda i: (0, 0)),
            pl.BlockSpec((1, QKV_COLS), lambda i: (0, 0)),
        ],
        out_specs=pl.BlockSpec((tm, QKV_COLS), lambda i: (i, 0)),
        out_shape=jax.ShapeDtypeStruct((n, QKV_COLS), BF16),
        compiler_params=pltpu.CompilerParams(
            dimension_semantics=("arbitrary",), vmem_limit_bytes=VMEM_LIMIT_BYTES),
        name="qkv_proj",
    )(x2d, g, w_qkv, col_scale)


def _sb_kernel(q_ref, k_ref, v_ref, u_ref, o_ref, acc_ref, carry_ref, z_ref, hilo_ref, arg_ref, *,
               t, w):
    qi = pl.program_id(2)
    rows = 2 * t
    top = (qi + 1) * (t // w) - 1
    q2 = _stack_halves(q_ref[0])
    row = lax.broadcasted_iota(jnp.int32, (rows, w), 0) & (t - 1)
    col = lax.broadcasted_iota(jnp.int32, (rows, w), 1)

    acc_ref[...] = jnp.zeros_like(acc_ref)
    carry_ref[...] = jnp.zeros_like(carry_ref)

    def scores(kb, masked, dst):
        kt = k_ref[0, pl.ds(pl.multiple_of(kb * w, w), w), :]
        z = lax.dot_general(q2, kt, _NT, preferred_element_type=F32)
        if masked:
            z = jnp.where(col + (kb * w - qi * t) < row, z, NEG_BIG)
        sp = jnp.maximum(z, jnp.log(1.0 + jnp.exp2(jnp.minimum(z, SOFTPLUS_CLAMP))) * LOG2E)
        hi = sp.astype(BF16)
        lo = (sp - hi.astype(F32)).astype(BF16)
        hilo_ref[dst] = jnp.concatenate([hi, lo], axis=1)
        z_ref[dst] = z

    def sums(src, dst):
        cum = jnp.dot(hilo_ref[src], u_ref[...], preferred_element_type=F32)
        c = carry_ref[...]
        arg_ref[dst] = z_ref[src] - cum - jnp.concatenate([c] * (w // LANES), axis=1)
        c = c + jnp.broadcast_to(cum[:, 0:1], (rows, LANES))
        carry_ref[...] = c
        return jnp.min(c)

    def values(kb, src):
        vt = v_ref[0, pl.ds(pl.multiple_of(kb * w, w), w), :]
        acc_ref[...] += jnp.dot(jnp.exp2(arg_ref[src]).astype(BF16), vt, preferred_element_type=F32)

    scores(top, True, 0)
    min_carry = sums(0, 0)
    scores(top - 1, True, 1)

    def live(state):
        p, min_carry = state
        return jnp.logical_and(p < (top - 1) // 2, min_carry < ZERO_WEIGHT_CARRY)

    def pair(state):
        p, _ = state
        kb = top - 2 * p
        scores(kb - 2, False, 0)
        values(kb, 0)
        sums(1, 1)
        scores(kb - 3, False, 1)
        values(kb - 1, 1)
        return p + 1, sums(0, 0)

    p, min_carry = lax.while_loop(live, pair, (jnp.int32(0), min_carry))
    kb = top - 2 * p
    values(kb, 0)

    @pl.when(min_carry < ZERO_WEIGHT_CARRY)
    def _():
        sums(1, 1)
        values(kb - 1, 1)

    lane = lax.broadcasted_iota(jnp.int32, (t, LANES), 1)
    o_ref[0] = jnp.where(lane < HEAD_DIM, acc_ref[:t], acc_ref[t:])


def _sb_attn(qkv, u2, t, w):
    b, s, _ = qkv.shape
    rows = 2 * t
    return pl.pallas_call(
        functools.partial(_sb_kernel, t=t, w=w),
        grid=(b, SB_HEADS // 2, s // t),
        in_specs=[
            pl.BlockSpec((1, t, LANES), lambda bi, p, qi: (bi, qi, SB_Q_BLK + p)),
            pl.BlockSpec((1, s, LANES), lambda bi, p, qi: (bi, 0, SB_K_BLK + p)),
            pl.BlockSpec((1, s, LANES), lambda bi, p, qi: (bi, 0, SB_V_BLK + p)),
            pl.BlockSpec((2 * w, w), lambda bi, p, qi: (0, 0)),
        ],
        out_specs=pl.BlockSpec((1, t, LANES), lambda bi, p, qi: (bi, qi, p)),
        out_shape=jax.ShapeDtypeStruct((b, s, SB_WIDTH), F32),
        scratch_shapes=[
            pltpu.VMEM((rows, LANES), F32),
            pltpu.VMEM((rows, LANES), F32),
            pltpu.VMEM((2, rows, w), F32),
            pltpu.VMEM((2, rows, 2 * w), BF16),
            pltpu.VMEM((2, rows, w), F32),
        ],
        compiler_params=pltpu.CompilerParams(
            dimension_semantics=("arbitrary", "arbitrary", "arbitrary"),
            vmem_limit_bytes=VMEM_LIMIT_BYTES),
        name="sb_attn",
    )(qkv, qkv, qkv, u2)


def _diff_kernel(q_ref, k_ref, v_ref, lam_ref, slope_ref, g_ref, o_ref, acc_ref, m_ref, alpha_ref,
                 s_ref, *, t, out_scale):
    hd = pl.program_id(1)
    qi = pl.program_id(2)
    rows = 2 * t
    reps = t // LANES
    q2 = _stack_halves(q_ref[0])
    slope2 = slope_ref[hd]
    key_iota = lax.broadcasted_iota(jnp.int32, (1, t), 1)
    ones = jnp.ones((t, LANES), BF16)

    acc_ref[...] = jnp.zeros_like(acc_ref)
    m_ref[...] = jnp.full_like(m_ref, NEG_BIG)

    def qk(kb):
        kt = k_ref[0, pl.ds(pl.multiple_of(kb * t, t), t), :]
        return lax.dot_general(q2, kt, _NT, preferred_element_type=F32)

    def scores(s2, kb, valid):
        bias = (key_iota + (kb - qi) * t).astype(F32) * slope2
        s2 = s2 + bias
        if valid is not None:
            s2 = jnp.where(valid, s2, NEG_BIG)
        m_old = m_ref[...]
        m_new = jnp.maximum(m_old, jnp.max(s2, axis=-1, keepdims=True))
        s_ref[...] = s2 - jnp.concatenate([m_new] * reps, axis=1)
        alpha_ref[...] = jnp.exp2(m_old - m_new)
        m_ref[...] = m_new

    def values(kb):
        vt = v_ref[0, pl.ds(pl.multiple_of(kb * t, t), t), :]
        alpha = alpha_ref[...]
        acc_ref[...] = (acc_ref[...] * jnp.concatenate([alpha, alpha], axis=1)
                        + jnp.dot(jnp.exp2(s_ref[...]).astype(BF16),
                                  jnp.concatenate([vt, ones], axis=1), preferred_element_type=F32))

    row = lax.broadcasted_iota(jnp.int32, (rows, t), 0) & (t - 1)
    col = lax.broadcasted_iota(jnp.int32, (rows, t), 1)
    scores(qk(qi), qi, col <= row)

    def body(it, _):
        s2 = qk(qi - 1 - it)
        values(qi - it)
        scores(s2, qi - 1 - it, None)
        return 0

    lax.fori_loop(0, qi, body, 0)
    values(0)

    a1 = acc_ref[:t]
    a2 = acc_ref[t:]
    o = a1[:, :LANES] / a1[:, LANES:] - lam_ref[0, 0] * (a2[:, :LANES] / a2[:, LANES:])
    o_ref[0] = _rms_norm(o, g_ref[...]) * out_scale


def _diff_attn(qkv, lam, subln_g, t, out_scale):
    b, s, _ = qkv.shape
    rows = 2 * t
    slopes2 = jnp.asarray([2.0 ** (-8.0 * (h + 1) / DIFF_HEADS) * LOG2E for h in range(DIFF_HEADS)], F32)
    return pl.pallas_call(
        functools.partial(_diff_kernel, t=t, out_scale=out_scale),
        grid=(b, DIFF_HEADS, s // t),
        in_specs=[
            pl.BlockSpec((1, t, LANES), lambda bi, h, qi: (bi, qi, DF_Q_BLK + h)),
            pl.BlockSpec((1, s, LANES), lambda bi, h, qi: (bi, 0, DF_K_BLK + h)),
            pl.BlockSpec((1, s, LANES), lambda bi, h, qi: (bi, 0, DF_V_BLK + h)),
            pl.BlockSpec(memory_space=pltpu.SMEM),
            pl.BlockSpec(memory_space=pltpu.SMEM),
            pl.BlockSpec((1, LANES), lambda bi, h, qi: (0, 0)),
        ],
        out_specs=pl.BlockSpec((1, t, LANES), lambda bi, h, qi: (bi, qi, h)),
        out_shape=jax.ShapeDtypeStruct((b, s, DIFF_WIDTH), F32),
        scratch_shapes=[
            pltpu.VMEM((rows, 2 * LANES), F32),
            pltpu.VMEM((rows, LANES), F32),
            pltpu.VMEM((rows, LANES), F32),
            pltpu.VMEM((rows, t), F32),
        ],
        compiler_params=pltpu.CompilerParams(
            dimension_semantics=("arbitrary", "arbitrary", "arbitrary"),
            vmem_limit_bytes=VMEM_LIMIT_BYTES),
        name="diff_attn",
    )(qkv, qkv, qkv, lam, slopes2, subln_g)


def _lambda_kernel(q1_ref, k1_ref, q2_ref, k2_ref, o_ref, *, lambda_init):
    s1 = jnp.sum(q1_ref[...] * k1_ref[...], axis=-1, keepdims=True)
    s2 = jnp.sum(q2_ref[...] * k2_ref[...], axis=-1, keepdims=True)
    o_ref[...] = jnp.exp(s1) - jnp.exp(s2) + lambda_init


def _lambda(q1, k1, q2, k2, lambda_init):
    return pl.pallas_call(
        functools.partial(_lambda_kernel, lambda_init=lambda_init),
        out_shape=jax.ShapeDtypeStruct((1, 1), F32),
        name="diff_lambda",
    )(q1, k1, q2, k2)


def _out_kernel(x_ref, sb_ref, df_ref, g_pre_ref, w_zg_ref, b_ref, w_sb_ref, w_df_ref, w_out_ref,
                g_post_ref, o_ref):
    x = x_ref[...]
    h = _rms_norm(x, g_pre_ref[...]).astype(BF16)
    zg = jnp.dot(h, w_zg_ref[...], preferred_element_type=F32)
    d = x.shape[-1]
    sb_z = zg[:, :SB_WIDTH]
    df_z = zg[:, SB_WIDTH:SB_WIDTH + DIFF_WIDTH]
    gates = jax.nn.sigmoid(zg[:, SB_WIDTH + DIFF_WIDTH:] + b_ref[...])
    y_sb = jnp.dot((sb_ref[...] * (sb_z * jax.nn.sigmoid(sb_z))).astype(BF16), w_sb_ref[...],
                   preferred_element_type=F32)
    y_df = jnp.dot((df_ref[...] * (df_z * jax.nn.sigmoid(df_z))).astype(BF16), w_df_ref[...],
                   preferred_element_type=F32)
    merged = gates[:, :d] * y_sb + gates[:, d:] * y_df
    out = jnp.dot(merged.astype(BF16), w_out_ref[...], preferred_element_type=F32)
    o_ref[...] = x + _rms_norm(out, g_post_ref[...])


def _out_proj(x2d, sb, df, g_pre, w_zg, b_gate, w_sb, w_df, w_out, g_post, tm):
    n, d = x2d.shape
    zg_cols = w_zg.shape[1]
    full = lambda shape: pl.BlockSpec(shape, lambda i: (0, 0))
    return pl.pallas_call(
        _out_kernel,
        grid=(n // tm,),
        in_specs=[
            pl.BlockSpec((tm, d), lambda i: (i, 0)),
            pl.BlockSpec((tm, SB_WIDTH), lambda i: (i, 0)),
            pl.BlockSpec((tm, DIFF_WIDTH), lambda i: (i, 0)),
            full((1, d)),
            full((d, zg_cols)),
            full((1, 2 * d)),
            full((SB_WIDTH, d)),
            full((DIFF_WIDTH, d)),
            full((d, d)),
            full((1, d)),
        ],
        out_specs=pl.BlockSpec((tm, d), lambda i: (i, 0)),
        out_shape=jax.ShapeDtypeStruct((n, d), F32),
        compiler_params=pltpu.CompilerParams(
            dimension_semantics=("arbitrary",), vmem_limit_bytes=VMEM_LIMIT_BYTES),
        name="out_proj",
    )(x2d, sb, df, g_pre, w_zg, b_gate, w_sb, w_df, w_out, g_post)


def _split_w_in(w):
    sizes = (SB_WIDTH,) * 4 + (DIFF_WIDTH,) * 4 + (w.shape[0],) * 2
    parts, off = [], 0
    for n in sizes:
        parts.append(w[:, off:off + n])
        off += n
    return parts


def kernel(x, pre_norm_g, w_in, b_gate, lambda_q1, lambda_k1, lambda_q2, lambda_k2, subln_g, w_o_sb,
           w_o_diff, w_out, post_norm_g):
    b, s, d = x.shape
    depth = w_in.shape[0]
    t, w = ATTN_T, SUM_W
    assert s % t == 0 and s >= 2 * t and t % w == 0
    tm = min(256, b * s)
    u = (lax.broadcasted_iota(jnp.int32, (w, w), 0)
         >= lax.broadcasted_iota(jnp.int32, (w, w), 1)).astype(BF16)
    u2 = jnp.concatenate([u, u], axis=0)
    q_scale = jnp.full((1, SB_WIDTH), QK_SCALE * LOG2E, F32)
    one = jnp.ones((1, SB_WIDTH), F32)
    col_scale = jnp.concatenate([q_scale, one, one, q_scale, one, one], axis=1)
    for layer in range(depth):
        sb_q, sb_k, sb_v, sb_z, df_q, df_k, df_v, df_z, g_sb, g_df = _split_w_in(w_in[layer])
        w_qkv = jnp.concatenate([sb_q, sb_k, sb_v, df_q, df_k, df_v], axis=1).astype(BF16)
        w_zg = jnp.concatenate([sb_z, df_z, g_sb, g_df], axis=1).astype(BF16)
        lambda_init = 0.8 - 0.6 * math.exp(-0.3 * layer)
        x2d = x.reshape(b * s, d)
        g_pre = pre_norm_g[layer].reshape(1, d)

        qkv = _qkv_proj(x2d, g_pre, w_qkv, col_scale, tm).reshape(b, s, QKV_COLS)
        sb_out = _sb_attn(qkv, u2, t, w)
        lam = _lambda(lambda_q1[layer][None], lambda_k1[layer][None], lambda_q2[layer][None],
                      lambda_k2[layer][None], lambda_init)
        df_out = _diff_attn(qkv, lam, subln_g[layer].reshape(1, LANES), t, 1.0 - lambda_init)
        x = _out_proj(x2d, sb_out.reshape(b * s, SB_WIDTH), df_out.reshape(b * s, DIFF_WIDTH), g_pre,
                      w_zg, b_gate[layer].reshape(1, 2 * d), w_o_sb[layer].astype(BF16),
                      w_o_diff[layer].astype(BF16), w_out[layer].astype(BF16),
                      post_norm_g[layer].reshape(1, d), tm).reshape(b, s, d)
    return x
```

```python
import functools
import math

import jax
import jax.numpy as jnp
from jax import lax
from jax.experimental import pallas as pl
from jax.experimental.pallas import tpu as pltpu

F32 = jnp.float32
BF16 = jnp.bfloat16

HEAD_DIM = 64
SB_HEADS = 8
DIFF_HEADS = 4
SB_WIDTH = SB_HEADS * HEAD_DIM
DIFF_WIDTH = DIFF_HEADS * 2 * HEAD_DIM
NORM_EPS = 1e-6
LANES = 128
MXU_DIM = 256
LOG2E = 1.4426950408889634
QK_SCALE = HEAD_DIM ** -0.5
NEG_BIG = -1e30
SOFTPLUS_CLAMP = 100.0
ZERO_WEIGHT_CARRY = 152.0
ZERO_WEIGHT_MARGIN = 156.0
VMEM_LIMIT_BYTES = 56 * 1024 * 1024

SUM_W = MXU_DIM
ATTN_T = 2 * SUM_W

QKV_COLS = 3 * SB_WIDTH + 3 * DIFF_WIDTH
SB_Q_BLK, SB_K_BLK, SB_V_BLK = 0, 4, 8
DF_Q_BLK, DF_K_BLK, DF_V_BLK = 12, 16, 20

_NT = (((1,), (1,)), ((), ()))


def _rms_norm(x32, g32):
    return x32 * lax.rsqrt(jnp.mean(x32 * x32, axis=-1, keepdims=True) + NORM_EPS) * g32


def _stack_halves(q):
    q32 = q.astype(F32)
    lane = lax.broadcasted_iota(jnp.int32, q.shape, 1)
    return jnp.concatenate([jnp.where(lane < HEAD_DIM, q32, 0.0).astype(BF16),
                            jnp.where(lane >= HEAD_DIM, q32, 0.0).astype(BF16)], axis=0)


def _qkv_kernel(x_ref, g_ref, w_ref, cs_ref, o_ref):
    h = _rms_norm(x_ref[...], g_ref[...]).astype(BF16)
    p = jnp.dot(h, w_ref[...], preferred_element_type=F32)
    o_ref[...] = (p * cs_ref[...]).astype(BF16)


def _qkv_proj(x2d, g, w_qkv, col_scale, tm):
    n, d = x2d.shape
    return pl.pallas_call(
        _qkv_kernel,
        grid=(n // tm,),
        in_specs=[
            pl.BlockSpec((tm, d), lambda i: (i, 0)),
            pl.BlockSpec((1, d), lambda i: (0, 0)),
            pl.BlockSpec((d, QKV_COLS), lambda i: (0, 0)),
            pl.BlockSpec((1, QKV_COLS), lambda i: (0, 0)),
        ],
        out_specs=pl.BlockSpec((tm, QKV_COLS), lambda i: (i, 0)),
        out_shape=jax.ShapeDtypeStruct((n, QKV_COLS), BF16),
        compiler_params=pltpu.CompilerParams(
            dimension_semantics=("arbitrary",), vmem_limit_bytes=VMEM_LIMIT_BYTES),
        name="qkv_proj",
    )(x2d, g, w_qkv, col_scale)


def _sb_kernel(q_ref, k_ref, v_ref, u_ref, o_ref, acc_ref, carry_ref, z_ref, hilo_ref, arg_ref, *,
               t, w):
    qi = pl.program_id(2)
    rows = 2 * t
    top = (qi + 1) * (t // w) - 1
    q2 = _stack_halves(q_ref[0])
    row = lax.broadcasted_iota(jnp.int32, (rows, w), 0) & (t - 1)
    col = lax.broadcasted_iota(jnp.int32, (rows, w), 1)

    acc_ref[...] = jnp.zeros_like(acc_ref)
    carry_ref[...] = jnp.zeros_like(carry_ref)

    def scores(kb, masked, dst):
        kt = k_ref[0, pl.ds(pl.multiple_of(kb * w, w), w), :]
        z = lax.dot_general(q2, kt, _NT, preferred_element_type=F32)
        if masked:
            z = jnp.where(col + (kb * w - qi * t) < row, z, NEG_BIG)
        sp = jnp.maximum(z, jnp.log(1.0 + jnp.exp2(jnp.minimum(z, SOFTPLUS_CLAMP))) * LOG2E)
        hi = sp.astype(BF16)
        lo = (sp - hi.astype(F32)).astype(BF16)
        hilo_ref[dst] = jnp.concatenate([hi, lo], axis=1)
        z_ref[dst] = z

    def sums(src, dst):
        cum = jnp.dot(hilo_ref[src], u_ref[...], preferred_element_type=F32)
        c = carry_ref[...]
        arg_ref[dst] = z_ref[src] - cum - jnp.concatenate([c] * (w // LANES), axis=1)
        c = c + jnp.broadcast_to(cum[:, 0:1], (rows, LANES))
        carry_ref[...] = c
        return jnp.min(c)

    def values(kb, src):
        vt = v_ref[0, pl.ds(pl.multiple_of(kb * w, w), w), :]
        acc_ref[...] += jnp.dot(jnp.exp2(arg_ref[src]).astype(BF16), vt, preferred_element_type=F32)

    scores(top, True, 0)
    min_carry = sums(0, 0)
    scores(top - 1, True, 1)

    def live(state):
        p, min_carry = state
        return jnp.logical_and(p < (top - 1) // 2, min_carry < ZERO_WEIGHT_CARRY)

    def pair(state):
        p, _ = state
        kb = top - 2 * p
        scores(kb - 2, False, 0)
        values(kb, 0)
        sums(1, 1)
        scores(kb - 3, False, 1)
        values(kb - 1, 1)
        return p + 1, sums(0, 0)

    p, min_carry = lax.while_loop(live, pair, (jnp.int32(0), min_carry))
    kb = top - 2 * p
    values(kb, 0)

    @pl.when(min_carry < ZERO_WEIGHT_CARRY)
    def _():
        sums(1, 1)
        values(kb - 1, 1)

    lane = lax.broadcasted_iota(jnp.int32, (t, LANES), 1)
    o_ref[0] = jnp.where(lane < HEAD_DIM, acc_ref[:t], acc_ref[t:])


def _sb_attn(qkv, u2, t, w):
    b, s, _ = qkv.shape
    rows = 2 * t
    return pl.pallas_call(
        functools.partial(_sb_kernel, t=t, w=w),
        grid=(b, SB_HEADS // 2, s // t),
        in_specs=[
            pl.BlockSpec((1, t, LANES), lambda bi, p, qi: (bi, qi, SB_Q_BLK + p)),
            pl.BlockSpec((1, s, LANES), lambda bi, p, qi: (bi, 0, SB_K_BLK + p)),
            pl.BlockSpec((1, s, LANES), lambda bi, p, qi: (bi, 0, SB_V_BLK + p)),
            pl.BlockSpec((2 * w, w), lambda bi, p, qi: (0, 0)),
        ],
        out_specs=pl.BlockSpec((1, t, LANES), lambda bi, p, qi: (bi, qi, p)),
        out_shape=jax.ShapeDtypeStruct((b, s, SB_WIDTH), F32),
        scratch_shapes=[
            pltpu.VMEM((rows, LANES), F32),
            pltpu.VMEM((rows, LANES), F32),
            pltpu.VMEM((2, rows, w), F32),
            pltpu.VMEM((2, rows, 2 * w), BF16),
            pltpu.VMEM((2, rows, w), F32),
        ],
        compiler_params=pltpu.CompilerParams(
            dimension_semantics=("arbitrary", "arbitrary", "arbitrary"),
            vmem_limit_bytes=VMEM_LIMIT_BYTES),
        name="sb_attn",
    )(qkv, qkv, qkv, u2)


def _diff_kernel(q_ref, k_ref, v_ref, lam_ref, slope_ref, g_ref, o_ref, acc_ref, m_ref, alpha_ref,
                 s_ref, knorm_ref, *, t, out_scale):
    hd = pl.program_id(1)
    qi = pl.program_id(2)
    rows = 2 * t
    reps = t // LANES
    q2 = _stack_halves(q_ref[0])
    slope2 = slope_ref[hd]
    key_iota = lax.broadcasted_iota(jnp.int32, (1, t), 1)
    ones = jnp.ones((t, LANES), BF16)

    acc_ref[...] = jnp.zeros_like(acc_ref)
    m_ref[...] = jnp.full_like(m_ref, NEG_BIG)

    def qk(kb):
        kt = k_ref[0, pl.ds(pl.multiple_of(kb * t, t), t), :]
        return lax.dot_general(q2, kt, _NT, preferred_element_type=F32)

    def scores(s2, kb, valid):
        bias = (key_iota + (kb - qi) * t).astype(F32) * slope2
        s2 = s2 + bias
        if valid is not None:
            s2 = jnp.where(valid, s2, NEG_BIG)
        m_old = m_ref[...]
        m_new = jnp.maximum(m_old, jnp.max(s2, axis=-1, keepdims=True))
        s_ref[...] = s2 - jnp.concatenate([m_new] * reps, axis=1)
        alpha_ref[...] = jnp.exp2(m_old - m_new)
        m_ref[...] = m_new

    def min_running_max():
        m = m_ref[...]
        return jnp.min(m[:t]), jnp.min(m[t:])

    def values(kb):
        vt = v_ref[0, pl.ds(pl.multiple_of(kb * t, t), t), :]
        alpha = alpha_ref[...]
        acc_ref[...] = (acc_ref[...] * jnp.concatenate([alpha, alpha], axis=1)
                        + jnp.dot(jnp.exp2(s_ref[...]).astype(BF16),
                                  jnp.concatenate([vt, ones], axis=1), preferred_element_type=F32))

    lane = lax.broadcasted_iota(jnp.int32, (t, LANES), 1)
    lo_half = lane < HEAD_DIM

    @pl.when(qi == 0)
    def _():
        def tile_norms(j, best):
            kt = k_ref[0, pl.ds(pl.multiple_of(j * t, t), t), :].astype(F32)
            sq = kt * kt
            n1 = jnp.max(jnp.sum(jnp.where(lo_half, sq, 0.0), axis=-1, keepdims=True))
            n2 = jnp.max(jnp.sum(jnp.where(lo_half, 0.0, sq), axis=-1, keepdims=True))
            best = (jnp.maximum(best[0], n1), jnp.maximum(best[1], n2))
            knorm_ref[0, j] = best[0]
            knorm_ref[1, j] = best[1]
            return best

        lax.fori_loop(0, k_ref.shape[1] // t, tile_norms, (jnp.float32(0.0), jnp.float32(0.0)))

    q_sq = q2.astype(F32)
    q_sq = q_sq * q_sq
    qn1 = jnp.sum(jnp.max(q_sq[:t], axis=0, keepdims=True))
    qn2 = jnp.sum(jnp.max(q_sq[t:], axis=0, keepdims=True))

    def all_zero(kb, mmin1, mmin2):
        bias_max = ((kb - qi) * t + (t - 1)).astype(F32) * slope2

        def below(qn, kn, mmin):
            room = mmin - ZERO_WEIGHT_MARGIN - bias_max
            return jnp.logical_and(room > 0.0, qn * kn <= room * room)

        return jnp.logical_and(below(qn1, knorm_ref[0, kb], mmin1), below(qn2, knorm_ref[1, kb], mmin2))

    row = lax.broadcasted_iota(jnp.int32, (rows, t), 0) & (t - 1)
    col = lax.broadcasted_iota(jnp.int32, (rows, t), 1)
    scores(qk(qi), qi, col <= row)
    mmin = min_running_max()

    def live(state):
        it, mmin1, mmin2 = state
        kb = jnp.maximum(qi - 1 - it, 0)
        return jnp.logical_and(it < qi, jnp.logical_not(all_zero(kb, mmin1, mmin2)))

    def step(state):
        it = state[0]
        s2 = qk(qi - 1 - it)
        mmin = min_running_max()
        values(qi - it)
        scores(s2, qi - 1 - it, None)
        return (it + 1,) + mmin

    done = lax.while_loop(live, step, (jnp.int32(0),) + mmin)
    values(qi - done[0])

    a1 = acc_ref[:t]
    a2 = acc_ref[t:]
    o = a1[:, :LANES] / a1[:, LANES:] - lam_ref[0, 0] * (a2[:, :LANES] / a2[:, LANES:])
    o_ref[0] = _rms_norm(o, g_ref[...]) * out_scale


def _diff_attn(qkv, lam, subln_g, t, out_scale):
    b, s, _ = qkv.shape
    rows = 2 * t
    slopes2 = jnp.asarray([2.0 ** (-8.0 * (h + 1) / DIFF_HEADS) * LOG2E for h in range(DIFF_HEADS)], F32)
    return pl.pallas_call(
        functools.partial(_diff_kernel, t=t, out_scale=out_scale),
        grid=(b, DIFF_HEADS, s // t),
        in_specs=[
            pl.BlockSpec((1, t, LANES), lambda bi, h, qi: (bi, qi, DF_Q_BLK + h)),
            pl.BlockSpec((1, s, LANES), lambda bi, h, qi: (bi, 0, DF_K_BLK + h)),
            pl.BlockSpec((1, s, LANES), lambda bi, h, qi: (bi, 0, DF_V_BLK + h)),
            pl.BlockSpec(memory_space=pltpu.SMEM),
            pl.BlockSpec(memory_space=pltpu.SMEM),
            pl.BlockSpec((1, LANES), lambda bi, h, qi: (0, 0)),
        ],
        out_specs=pl.BlockSpec((1, t, LANES), lambda bi, h, qi: (bi, qi, h)),
        out_shape=jax.ShapeDtypeStruct((b, s, DIFF_WIDTH), F32),
        scratch_shapes=[
            pltpu.VMEM((rows, 2 * LANES), F32),
            pltpu.VMEM((rows, LANES), F32),
            pltpu.VMEM((rows, LANES), F32),
            pltpu.VMEM((rows, t), F32),
            pltpu.SMEM((2, s // t), F32),
        ],
        compiler_params=pltpu.CompilerParams(
            dimension_semantics=("arbitrary", "arbitrary", "arbitrary"),
            vmem_limit_bytes=VMEM_LIMIT_BYTES),
        name="diff_attn",
    )(qkv, qkv, qkv, lam, slopes2, subln_g)


def _lambda_kernel(q1_ref, k1_ref, q2_ref, k2_ref, o_ref, *, lambda_init):
    s1 = jnp.sum(q1_ref[...] * k1_ref[...], axis=-1, keepdims=True)
    s2 = jnp.sum(q2_ref[...] * k2_ref[...], axis=-1, keepdims=True)
    o_ref[...] = jnp.exp(s1) - jnp.exp(s2) + lambda_init


def _lambda(q1, k1, q2, k2, lambda_init):
    return pl.pallas_call(
        functools.partial(_lambda_kernel, lambda_init=lambda_init),
        out_shape=jax.ShapeDtypeStruct((1, 1), F32),
        name="diff_lambda",
    )(q1, k1, q2, k2)


def _out_kernel(x_ref, sb_ref, df_ref, g_pre_ref, w_zg_ref, b_ref, w_sb_ref, w_df_ref, w_out_ref,
                g_post_ref, o_ref):
    x = x_ref[...]
    h = _rms_norm(x, g_pre_ref[...]).astype(BF16)
    zg = jnp.dot(h, w_zg_ref[...], preferred_element_type=F32)
    d = x.shape[-1]
    sb_z = zg[:, :SB_WIDTH]
    df_z = zg[:, SB_WIDTH:SB_WIDTH + DIFF_WIDTH]
    gates = jax.nn.sigmoid(zg[:, SB_WIDTH + DIFF_WIDTH:] + b_ref[...])
    y_sb = jnp.dot((sb_ref[...] * (sb_z * jax.nn.sigmoid(sb_z))).astype(BF16), w_sb_ref[...],
                   preferred_element_type=F32)
    y_df = jnp.dot((df_ref[...] * (df_z * jax.nn.sigmoid(df_z))).astype(BF16), w_df_ref[...],
                   preferred_element_type=F32)
    merged = gates[:, :d] * y_sb + gates[:, d:] * y_df
    out = jnp.dot(merged.astype(BF16), w_out_ref[...], preferred_element_type=F32)
    o_ref[...] = x + _rms_norm(out, g_post_ref[...])


def _out_proj(x2d, sb, df, g_pre, w_zg, b_gate, w_sb, w_df, w_out, g_post, tm):
    n, d = x2d.shape
    zg_cols = w_zg.shape[1]
    full = lambda shape: pl.BlockSpec(shape, lambda i: (0, 0))
    return pl.pallas_call(
        _out_kernel,
        grid=(n // tm,),
        in_specs=[
            pl.BlockSpec((tm, d), lambda i: (i, 0)),
            pl.BlockSpec((tm, SB_WIDTH), lambda i: (i, 0)),
            pl.BlockSpec((tm, DIFF_WIDTH), lambda i: (i, 0)),
            full((1, d)),
            full((d, zg_cols)),
            full((1, 2 * d)),
            full((SB_WIDTH, d)),
            full((DIFF_WIDTH, d)),
            full((d, d)),
            full((1, d)),
        ],
        out_specs=pl.BlockSpec((tm, d), lambda i: (i, 0)),
        out_shape=jax.ShapeDtypeStruct((n, d), F32),
        compiler_params=pltpu.CompilerParams(
            dimension_semantics=("arbitrary",), vmem_limit_bytes=VMEM_LIMIT_BYTES),
        name="out_proj",
    )(x2d, sb, df, g_pre, w_zg, b_gate, w_sb, w_df, w_out, g_post)


def _split_w_in(w):
    sizes = (SB_WIDTH,) * 4 + (DIFF_WIDTH,) * 4 + (w.shape[0],) * 2
    parts, off = [], 0
    for n in sizes:
        parts.append(w[:, off:off + n])
        off += n
    return parts


def kernel(x, pre_norm_g, w_in, b_gate, lambda_q1, lambda_k1, lambda_q2, lambda_k2, subln_g, w_o_sb,
           w_o_diff, w_out, post_norm_g):
    b, s, d = x.shape
    depth = w_in.shape[0]
    t, w = ATTN_T, SUM_W
    assert s % t == 0 and s >= 2 * t and t % w == 0
    tm = min(256, b * s)
    u = (lax.broadcasted_iota(jnp.int32, (w, w), 0)
         >= lax.broadcasted_iota(jnp.int32, (w, w), 1)).astype(BF16)
    u2 = jnp.concatenate([u, u], axis=0)
    q_scale = jnp.full((1, SB_WIDTH), QK_SCALE * LOG2E, F32)
    one = jnp.ones((1, SB_WIDTH), F32)
    col_scale = jnp.concatenate([q_scale, one, one, q_scale, one, one], axis=1)
    for layer in range(depth):
        sb_q, sb_k, sb_v, sb_z, df_q, df_k, df_v, df_z, g_sb, g_df = _split_w_in(w_in[layer])
        w_qkv = jnp.concatenate([sb_q, sb_k, sb_v, df_q, df_k, df_v], axis=1).astype(BF16)
        w_zg = jnp.concatenate([sb_z, df_z, g_sb, g_df], axis=1).astype(BF16)
        lambda_init = 0.8 - 0.6 * math.exp(-0.3 * layer)
        x2d = x.reshape(b * s, d)
        g_pre = pre_norm_g[layer].reshape(1, d)

        qkv = _qkv_proj(x2d, g_pre, w_qkv, col_scale, tm).reshape(b, s, QKV_COLS)
        sb_out = _sb_attn(qkv, u2, t, w)
        lam = _lambda(lambda_q1[layer][None], lambda_k1[layer][None], lambda_q2[layer][None],
                      lambda_k2[layer][None], lambda_init)
        df_out = _diff_attn(qkv, lam, subln_g[layer].reshape(1, LANES), t, 1.0 - lambda_init)
        x = _out_proj(x2d, sb_out.reshape(b * s, SB_WIDTH), df_out.reshape(b * s, DIFF_WIDTH), g_pre,
                      w_zg, b_gate[layer].reshape(1, 2 * d), w_o_sb[layer].astype(BF16),
                      w_o_diff[layer].astype(BF16), w_out[layer].astype(BF16),
                      post_norm_g[layer].reshape(1, d), tm).reshape(b, s, d)
    return x
```

```python
import functools
import math

import jax
import jax.numpy as jnp
from jax import lax
from jax.experimental import pallas as pl
from jax.experimental.pallas import tpu as pltpu

F32 = jnp.float32
BF16 = jnp.bfloat16

HEAD_DIM = 64
SB_HEADS = 8
DIFF_HEADS = 4
SB_WIDTH = SB_HEADS * HEAD_DIM
DIFF_WIDTH = DIFF_HEADS * 2 * HEAD_DIM
NORM_EPS = 1e-6
LANES = 128
MXU_DIM = 256
LOG2E = 1.4426950408889634
QK_SCALE = HEAD_DIM ** -0.5
NEG_BIG = -1e30
SOFTPLUS_CLAMP = 100.0
ZERO_WEIGHT_CARRY = 152.0
ZERO_WEIGHT_MARGIN = 156.0
VMEM_LIMIT_BYTES = 56 * 1024 * 1024

SUM_W = MXU_DIM
ATTN_T = 2 * SUM_W

QKV_COLS = 3 * SB_WIDTH + 3 * DIFF_WIDTH
SB_Q_BLK, SB_K_BLK, SB_V_BLK = 0, 4, 8
DF_Q_BLK, DF_K_BLK, DF_V_BLK = 12, 16, 20

_NT = (((1,), (1,)), ((), ()))


def _rms_norm(x32, g32):
    return x32 * lax.rsqrt(jnp.mean(x32 * x32, axis=-1, keepdims=True) + NORM_EPS) * g32


def _stack_halves(q):
    q32 = q.astype(F32)
    lane = lax.broadcasted_iota(jnp.int32, q.shape, 1)
    return jnp.concatenate([jnp.where(lane < HEAD_DIM, q32, 0.0).astype(BF16),
                            jnp.where(lane >= HEAD_DIM, q32, 0.0).astype(BF16)], axis=0)


def _qkv_kernel(x_ref, g_ref, w_ref, cs_ref, o_ref):
    h = _rms_norm(x_ref[...], g_ref[...]).astype(BF16)
    p = jnp.dot(h, w_ref[...], preferred_element_type=F32)
    o_ref[...] = (p * cs_ref[...]).astype(BF16)


def _qkv_proj(x2d, g, w_qkv, col_scale, tm):
    n, d = x2d.shape
    return pl.pallas_call(
        _qkv_kernel,
        grid=(n // tm,),
        in_specs=[
            pl.BlockSpec((tm, d), lambda i: (i, 0)),
            pl.BlockSpec((1, d), lambda i: (0, 0)),
            pl.BlockSpec((d, QKV_COLS), lambda i: (0, 0)),
            pl.BlockSpec((1, QKV_COLS), lambda i: (0, 0)),
        ],
        out_specs=pl.BlockSpec((tm, QKV_COLS), lambda i: (i, 0)),
        out_shape=jax.ShapeDtypeStruct((n, QKV_COLS), BF16),
        compiler_params=pltpu.CompilerParams(
            dimension_semantics=("arbitrary",), vmem_limit_bytes=VMEM_LIMIT_BYTES),
        name="qkv_proj",
    )(x2d, g, w_qkv, col_scale)


def _sb_kernel(q_ref, k_ref, v_ref, u_ref, o_ref, acc_ref, carry_ref, z_ref, hilo_ref, arg_ref, *,
               t, w):
    qi = pl.program_id(2)
    rows = 2 * t
    top = (qi + 1) * (t // w) - 1
    q2 = _stack_halves(q_ref[0])
    row = lax.broadcasted_iota(jnp.int32, (rows, w), 0) & (t - 1)
    col = lax.broadcasted_iota(jnp.int32, (rows, w), 1)

    acc_ref[...] = jnp.zeros_like(acc_ref)
    carry_ref[...] = jnp.zeros_like(carry_ref)

    def scores(kb, masked, dst):
        kt = k_ref[0, pl.ds(pl.multiple_of(kb * w, w), w), :]
        z = lax.dot_general(q2, kt, _NT, preferred_element_type=F32)
        if masked:
            z = jnp.where(col + (kb * w - qi * t) < row, z, NEG_BIG)
        sp = jnp.maximum(z, jnp.log(1.0 + jnp.exp2(jnp.minimum(z, SOFTPLUS_CLAMP))) * LOG2E)
        hi = sp.astype(BF16)
        lo = (sp - hi.astype(F32)).astype(BF16)
        hilo_ref[dst] = jnp.concatenate([hi, lo], axis=1)
        z_ref[dst] = z

    def sums(src, dst):
        cum = jnp.dot(hilo_ref[src], u_ref[...], preferred_element_type=F32)
        c = carry_ref[...]
        arg_ref[dst] = z_ref[src] - cum - jnp.concatenate([c] * (w // LANES), axis=1)
        c = c + jnp.broadcast_to(cum[:, 0:1], (rows, LANES))
        carry_ref[...] = c
        return jnp.min(c)

    def values(kb, src):
        vt = v_ref[0, pl.ds(pl.multiple_of(kb * w, w), w), :]
        acc_ref[...] += jnp.dot(jnp.exp2(arg_ref[src]).astype(BF16), vt, preferred_element_type=F32)

    scores(top, True, 0)
    min_carry = sums(0, 0)
    scores(top - 1, True, 1)

    def live(state):
        p, min_carry = state
        return jnp.logical_and(p < (top - 1) // 2, min_carry < ZERO_WEIGHT_CARRY)

    def pair(state):
        p, _ = state
        kb = top - 2 * p
        scores(kb - 2, False, 0)
        values(kb, 0)
        sums(1, 1)
        scores(kb - 3, False, 1)
        values(kb - 1, 1)
        return p + 1, sums(0, 0)

    p, min_carry = lax.while_loop(live, pair, (jnp.int32(0), min_carry))
    kb = top - 2 * p
    values(kb, 0)

    @pl.when(min_carry < ZERO_WEIGHT_CARRY)
    def _():
        sums(1, 1)
        values(kb - 1, 1)

    lane = lax.broadcasted_iota(jnp.int32, (t, LANES), 1)
    o_ref[0] = jnp.where(lane < HEAD_DIM, acc_ref[:t], acc_ref[t:])


def _sb_attn(qkv, u2, t, w):
    b, s, _ = qkv.shape
    rows = 2 * t
    return pl.pallas_call(
        functools.partial(_sb_kernel, t=t, w=w),
        grid=(b, SB_HEADS // 2, s // t),
        in_specs=[
            pl.BlockSpec((1, t, LANES), lambda bi, p, qi: (bi, qi, SB_Q_BLK + p)),
            pl.BlockSpec((1, s, LANES), lambda bi, p, qi: (bi, 0, SB_K_BLK + p)),
            pl.BlockSpec((1, s, LANES), lambda bi, p, qi: (bi, 0, SB_V_BLK + p)),
            pl.BlockSpec((2 * w, w), lambda bi, p, qi: (0, 0)),
        ],
        out_specs=pl.BlockSpec((1, t, LANES), lambda bi, p, qi: (bi, qi, p)),
        out_shape=jax.ShapeDtypeStruct((b, s, SB_WIDTH), F32),
        scratch_shapes=[
            pltpu.VMEM((rows, LANES), F32),
            pltpu.VMEM((rows, LANES), F32),
            pltpu.VMEM((2, rows, w), F32),
            pltpu.VMEM((2, rows, 2 * w), BF16),
            pltpu.VMEM((2, rows, w), F32),
        ],
        compiler_params=pltpu.CompilerParams(
            dimension_semantics=("arbitrary", "arbitrary", "arbitrary"),
            vmem_limit_bytes=VMEM_LIMIT_BYTES),
        name="sb_attn",
    )(qkv, qkv, qkv, u2)


def _diff_kernel(q_ref, k_ref, v_ref, lam_ref, slope_ref, g_ref, o_ref, acc_ref, m_ref, alpha_ref,
                 mtile_ref, s_ref, knorm_ref, *, t, out_scale):
    hd = pl.program_id(1)
    qi = pl.program_id(2)
    rows = 2 * t
    reps = t // LANES
    q2 = _stack_halves(q_ref[0])
    slope2 = slope_ref[hd]
    key_iota = lax.broadcasted_iota(jnp.int32, (1, t), 1)
    ones = jnp.ones((t, LANES), BF16)

    acc_ref[...] = jnp.zeros_like(acc_ref)
    m_ref[...] = jnp.full_like(m_ref, NEG_BIG)

    def qk(kb):
        kt = k_ref[0, pl.ds(pl.multiple_of(kb * t, t), t), :]
        return lax.dot_general(q2, kt, _NT, preferred_element_type=F32)

    def scores(kb, valid, dst):
        bias = (key_iota + (kb - qi) * t).astype(F32) * slope2
        s2 = qk(kb) + bias
        if valid is not None:
            s2 = jnp.where(valid, s2, NEG_BIG)
        m_old = m_ref[...]
        m_new = jnp.maximum(m_old, jnp.max(s2, axis=-1, keepdims=True))
        s_ref[dst] = s2
        mtile_ref[dst] = m_new
        alpha_ref[dst] = jnp.exp2(m_old - m_new)
        m_ref[...] = m_new

    def min_running_max():
        m = m_ref[...]
        return jnp.min(m[:t]), jnp.min(m[t:])

    def values(kb, src):
        vt = v_ref[0, pl.ds(pl.multiple_of(kb * t, t), t), :]
        alpha = alpha_ref[src]
        p = jnp.exp2(s_ref[src] - jnp.concatenate([mtile_ref[src]] * reps, axis=1))
        acc_ref[...] = (acc_ref[...] * jnp.concatenate([alpha, alpha], axis=1)
                        + jnp.dot(p.astype(BF16), jnp.concatenate([vt, ones], axis=1),
                                  preferred_element_type=F32))

    lane = lax.broadcasted_iota(jnp.int32, (t, LANES), 1)
    lo_half = lane < HEAD_DIM

    @pl.when(qi == 0)
    def _():
        def tile_norms(j, best):
            kt = k_ref[0, pl.ds(pl.multiple_of(j * t, t), t), :].astype(F32)
            sq = kt * kt
            n1 = jnp.max(jnp.sum(jnp.where(lo_half, sq, 0.0), axis=-1, keepdims=True))
            n2 = jnp.max(jnp.sum(jnp.where(lo_half, 0.0, sq), axis=-1, keepdims=True))
            best = (jnp.maximum(best[0], n1), jnp.maximum(best[1], n2))
            knorm_ref[0, j] = best[0]
            knorm_ref[1, j] = best[1]
            return best

        lax.fori_loop(0, k_ref.shape[1] // t, tile_norms, (jnp.float32(0.0), jnp.float32(0.0)))

    q_sq = q2.astype(F32)
    q_sq = q_sq * q_sq
    qn1 = jnp.sum(jnp.max(q_sq[:t], axis=0, keepdims=True))
    qn2 = jnp.sum(jnp.max(q_sq[t:], axis=0, keepdims=True))

    def all_zero(kb, mmin1, mmin2):
        bias_max = ((kb - qi) * t + (t - 1)).astype(F32) * slope2

        def below(qn, kn, mmin):
            room = mmin - ZERO_WEIGHT_MARGIN - bias_max
            return jnp.logical_and(room > 0.0, qn * kn <= room * room)

        return jnp.logical_and(below(qn1, knorm_ref[0, kb], mmin1), below(qn2, knorm_ref[1, kb], mmin2))

    row = lax.broadcasted_iota(jnp.int32, (rows, t), 0) & (t - 1)
    col = lax.broadcasted_iota(jnp.int32, (rows, t), 1)
    scores(qi, col <= row, 0)

    def needed(j, mmin):
        kb = jnp.maximum(qi - j, 0)
        return jnp.logical_and(j <= qi, jnp.logical_not(all_zero(kb, *mmin)))

    def live(state):
        return needed(state[0] + 2, state[1:])

    def pair(state):
        j = state[0]
        mmin = min_running_max()
        scores(qi - j - 1, None, 1)
        values(qi - j, 0)
        scores(qi - j - 2, None, 0)
        values(qi - j - 1, 1)
        return (j + 2,) + mmin

    state = lax.while_loop(live, pair, (jnp.int32(0),) + min_running_max())
    j = state[0]
    one_more = needed(j + 1, min_running_max())

    @pl.when(one_more)
    def _():
        scores(qi - j - 1, None, 1)
        values(qi - j, 0)
        values(qi - j - 1, 1)

    @pl.when(jnp.logical_not(one_more))
    def _():
        values(qi - j, 0)

    a1 = acc_ref[:t]
    a2 = acc_ref[t:]
    o = a1[:, :LANES] / a1[:, LANES:] - lam_ref[0, 0] * (a2[:, :LANES] / a2[:, LANES:])
    o_ref[0] = _rms_norm(o, g_ref[...]) * out_scale


def _diff_attn(qkv, lam, subln_g, t, out_scale):
    b, s, _ = qkv.shape
    rows = 2 * t
    slopes2 = jnp.asarray([2.0 ** (-8.0 * (h + 1) / DIFF_HEADS) * LOG2E for h in range(DIFF_HEADS)], F32)
    return pl.pallas_call(
        functools.partial(_diff_kernel, t=t, out_scale=out_scale),
        grid=(b, DIFF_HEADS, s // t),
        in_specs=[
            pl.BlockSpec((1, t, LANES), lambda bi, h, qi: (bi, qi, DF_Q_BLK + h)),
            pl.BlockSpec((1, s, LANES), lambda bi, h, qi: (bi, 0, DF_K_BLK + h)),
            pl.BlockSpec((1, s, LANES), lambda bi, h, qi: (bi, 0, DF_V_BLK + h)),
            pl.BlockSpec(memory_space=pltpu.SMEM),
            pl.BlockSpec(memory_space=pltpu.SMEM),
            pl.BlockSpec((1, LANES), lambda bi, h, qi: (0, 0)),
        ],
        out_specs=pl.BlockSpec((1, t, LANES), lambda bi, h, qi: (bi, qi, h)),
        out_shape=jax.ShapeDtypeStruct((b, s, DIFF_WIDTH), F32),
        scratch_shapes=[
            pltpu.VMEM((rows, 2 * LANES), F32),
            pltpu.VMEM((rows, LANES), F32),
            pltpu.VMEM((2, rows, LANES), F32),
            pltpu.VMEM((2, rows, LANES), F32),
            pltpu.VMEM((2, rows, t), F32),
            pltpu.SMEM((2, s // t), F32),
        ],
        compiler_params=pltpu.CompilerParams(
            dimension_semantics=("arbitrary", "arbitrary", "arbitrary"),
            vmem_limit_bytes=VMEM_LIMIT_BYTES),
        name="diff_attn",
    )(qkv, qkv, qkv, lam, slopes2, subln_g)


def _lambda_kernel(q1_ref, k1_ref, q2_ref, k2_ref, o_ref, *, lambda_init):
    s1 = jnp.sum(q1_ref[...] * k1_ref[...], axis=-1, keepdims=True)
    s2 = jnp.sum(q2_ref[...] * k2_ref[...], axis=-1, keepdims=True)
    o_ref[...] = jnp.exp(s1) - jnp.exp(s2) + lambda_init


def _lambda(q1, k1, q2, k2, lambda_init):
    return pl.pallas_call(
        functools.partial(_lambda_kernel, lambda_init=lambda_init),
        out_shape=jax.ShapeDtypeStruct((1, 1), F32),
        name="diff_lambda",
    )(q1, k1, q2, k2)


def _out_kernel(x_ref, sb_ref, df_ref, g_pre_ref, w_zg_ref, b_ref, w_sb_ref, w_df_ref, w_out_ref,
                g_post_ref, o_ref):
    x = x_ref[...]
    h = _rms_norm(x, g_pre_ref[...]).astype(BF16)
    zg = jnp.dot(h, w_zg_ref[...], preferred_element_type=F32)
    d = x.shape[-1]
    sb_z = zg[:, :SB_WIDTH]
    df_z = zg[:, SB_WIDTH:SB_WIDTH + DIFF_WIDTH]
    gates = jax.nn.sigmoid(zg[:, SB_WIDTH + DIFF_WIDTH:] + b_ref[...])
    y_sb = jnp.dot((sb_ref[...] * (sb_z * jax.nn.sigmoid(sb_z))).astype(BF16), w_sb_ref[...],
                   preferred_element_type=F32)
    y_df = jnp.dot((df_ref[...] * (df_z * jax.nn.sigmoid(df_z))).astype(BF16), w_df_ref[...],
                   preferred_element_type=F32)
    merged = gates[:, :d] * y_sb + gates[:, d:] * y_df
    out = jnp.dot(merged.astype(BF16), w_out_ref[...], preferred_element_type=F32)
    o_ref[...] = x + _rms_norm(out, g_post_ref[...])


def _out_proj(x2d, sb, df, g_pre, w_zg, b_gate, w_sb, w_df, w_out, g_post, tm):
    n, d = x2d.shape
    zg_cols = w_zg.shape[1]
    full = lambda shape: pl.BlockSpec(shape, lambda i: (0, 0))
    return pl.pallas_call(
        _out_kernel,
        grid=(n // tm,),
        in_specs=[
            pl.BlockSpec((tm, d), lambda i: (i, 0)),
            pl.BlockSpec((tm, SB_WIDTH), lambda i: (i, 0)),
            pl.BlockSpec((tm, DIFF_WIDTH), lambda i: (i, 0)),
            full((1, d)),
            full((d, zg_cols)),
            full((1, 2 * d)),
            full((SB_WIDTH, d)),
            full((DIFF_WIDTH, d)),
            full((d, d)),
            full((1, d)),
        ],
        out_specs=pl.BlockSpec((tm, d), lambda i: (i, 0)),
        out_shape=jax.ShapeDtypeStruct((n, d), F32),
        compiler_params=pltpu.CompilerParams(
            dimension_semantics=("arbitrary",), vmem_limit_bytes=VMEM_LIMIT_BYTES),
        name="out_proj",
    )(x2d, sb, df, g_pre, w_zg, b_gate, w_sb, w_df, w_out, g_post)


def _split_w_in(w):
    sizes = (SB_WIDTH,) * 4 + (DIFF_WIDTH,) * 4 + (w.shape[0],) * 2
    parts, off = [], 0
    for n in sizes:
        parts.append(w[:, off:off + n])
        off += n
    return parts


def kernel(x, pre_norm_g, w_in, b_gate, lambda_q1, lambda_k1, lambda_q2, lambda_k2, subln_g, w_o_sb,
           w_o_diff, w_out, post_norm_g):
    b, s, d = x.shape
    depth = w_in.shape[0]
    t, w = ATTN_T, SUM_W
    assert s % t == 0 and s >= 2 * t and t % w == 0
    tm = min(256, b * s)
    u = (lax.broadcasted_iota(jnp.int32, (w, w), 0)
         >= lax.broadcasted_iota(jnp.int32, (w, w), 1)).astype(BF16)
    u2 = jnp.concatenate([u, u], axis=0)
    q_scale = jnp.full((1, SB_WIDTH), QK_SCALE * LOG2E, F32)
    one = jnp.ones((1, SB_WIDTH), F32)
    col_scale = jnp.concatenate([q_scale, one, one, q_scale, one, one], axis=1)
    for layer in range(depth):
        sb_q, sb_k, sb_v, sb_z, df_q, df_k, df_v, df_z, g_sb, g_df = _split_w_in(w_in[layer])
        w_qkv = jnp.concatenate([sb_q, sb_k, sb_v, df_q, df_k, df_v], axis=1).astype(BF16)
        w_zg = jnp.concatenate([sb_z, df_z, g_sb, g_df], axis=1).astype(BF16)
        lambda_init = 0.8 - 0.6 * math.exp(-0.3 * layer)
        x2d = x.reshape(b * s, d)
        g_pre = pre_norm_g[layer].reshape(1, d)

        qkv = _qkv_proj(x2d, g_pre, w_qkv, col_scale, tm).reshape(b, s, QKV_COLS)
        sb_out = _sb_attn(qkv, u2, t, w)
        lam = _lambda(lambda_q1[layer][None], lambda_k1[layer][None], lambda_q2[layer][None],
                      lambda_k2[layer][None], lambda_init)
        df_out = _diff_attn(qkv, lam, subln_g[layer].reshape(1, LANES), t, 1.0 - lambda_init)
        x = _out_proj(x2d, sb_out.reshape(b * s, SB_WIDTH), df_out.reshape(b * s, DIFF_WIDTH), g_pre,
                      w_zg, b_gate[layer].reshape(1, 2 * d), w_o_sb[layer].astype(BF16),
                      w_o_diff[layer].astype(BF16), w_out[layer].astype(BF16),
                      post_norm_g[layer].reshape(1, d), tm).reshape(b, s, d)
    return x
```

```python
import functools
import math

import jax
import jax.numpy as jnp
from jax import lax
from jax.experimental import pallas as pl
from jax.experimental.pallas import tpu as pltpu

F32 = jnp.float32
BF16 = jnp.bfloat16

HEAD_DIM = 64
SB_HEADS = 8
DIFF_HEADS = 4
SB_WIDTH = SB_HEADS * HEAD_DIM
DIFF_WIDTH = DIFF_HEADS * 2 * HEAD_DIM
NORM_EPS = 1e-6
LANES = 128
MXU_DIM = 256
LOG2E = 1.4426950408889634
QK_SCALE = HEAD_DIM ** -0.5
NEG_BIG = -1e30
SOFTPLUS_CLAMP = 100.0
ZERO_WEIGHT_CARRY = 152.0
ZERO_WEIGHT_MARGIN = 156.0
VMEM_LIMIT_BYTES = 56 * 1024 * 1024

SUM_W = MXU_DIM
ATTN_T = 2 * SUM_W

QKV_COLS = 3 * SB_WIDTH + 3 * DIFF_WIDTH
SB_Q_BLK, SB_K_BLK, SB_V_BLK = 0, 4, 8
DF_Q_BLK, DF_K_BLK, DF_V_BLK = 12, 16, 20

_NT = (((1,), (1,)), ((), ()))


def _rms_norm(x32, g32):
    return x32 * lax.rsqrt(jnp.mean(x32 * x32, axis=-1, keepdims=True) + NORM_EPS) * g32


def _stack_halves(q):
    q32 = q.astype(F32)
    lane = lax.broadcasted_iota(jnp.int32, q.shape, 1)
    return jnp.concatenate([jnp.where(lane < HEAD_DIM, q32, 0.0).astype(BF16),
                            jnp.where(lane >= HEAD_DIM, q32, 0.0).astype(BF16)], axis=0)


def _qkv_kernel(x_ref, g_ref, w_ref, cs_ref, o_ref):
    h = _rms_norm(x_ref[...], g_ref[...]).astype(BF16)
    p = jnp.dot(h, w_ref[...], preferred_element_type=F32)
    o_ref[...] = (p * cs_ref[...]).astype(BF16)


def _qkv_proj(x2d, g, w_qkv, col_scale, tm):
    n, d = x2d.shape
    return pl.pallas_call(
        _qkv_kernel,
        grid=(n // tm,),
        in_specs=[
            pl.BlockSpec((tm, d), lambda i: (i, 0)),
            pl.BlockSpec((1, d), lambda i: (0, 0)),
            pl.BlockSpec((d, QKV_COLS), lambda i: (0, 0)),
            pl.BlockSpec((1, QKV_COLS), lambda i: (0, 0)),
        ],
        out_specs=pl.BlockSpec((tm, QKV_COLS), lambda i: (i, 0)),
        out_shape=jax.ShapeDtypeStruct((n, QKV_COLS), BF16),
        compiler_params=pltpu.CompilerParams(
            dimension_semantics=("arbitrary",), vmem_limit_bytes=VMEM_LIMIT_BYTES),
        name="qkv_proj",
    )(x2d, g, w_qkv, col_scale)


def _sb_kernel(q_ref, k_ref, v_ref, u_ref, o_ref, acc_ref, carry_ref, z_ref, hilo_ref, arg_ref, *,
               t, w):
    qi = pl.program_id(2)
    rows = 2 * t
    top = (qi + 1) * (t // w) - 1
    q2 = _stack_halves(q_ref[0])
    row = lax.broadcasted_iota(jnp.int32, (rows, w), 0) & (t - 1)
    col = lax.broadcasted_iota(jnp.int32, (rows, w), 1)

    done = _sb_band(qi, q2, k_ref, v_ref, u_ref, o_ref, w=w)

    @pl.when(jnp.logical_not(done))
    def _():
        _sb_walk(qi, q2, row, col, top, k_ref, v_ref, u_ref, o_ref, acc_ref, carry_ref, z_ref, hilo_ref,
                 arg_ref, t=t, w=w)


def _softplus2(z):
    return jnp.maximum(z, jnp.log(1.0 + jnp.exp2(jnp.minimum(z, SOFTPLUS_CLAMP))) * LOG2E)


def _sb_band(qi, q2, k_ref, v_ref, u_ref, o_ref, *, w):
    t = 2 * w

    def block(ref, kb):
        return ref[0, pl.ds(pl.multiple_of(kb * w, w), w), :]

    b0 = 2 * qi
    bm = jnp.maximum(b0 - 1, 0)
    q_a = jnp.concatenate([q2[0:w], q2[t:t + w]], axis=0)
    q_b = jnp.concatenate([q2[w:t], q2[t + w:2 * t]], axis=0)
    z_0 = lax.dot_general(jnp.concatenate([q_a, q_b], axis=0), block(k_ref, b0), _NT,
                          preferred_element_type=F32)
    z_b_diag = lax.dot_general(q_b, block(k_ref, b0 + 1), _NT, preferred_element_type=F32)
    z_a_left = lax.dot_general(q_a, block(k_ref, bm), _NT, preferred_element_type=F32)
    causal = (lax.broadcasted_iota(jnp.int32, (t, w), 1)
              < (lax.broadcasted_iota(jnp.int32, (t, w), 0) & (w - 1)))
    z_first = jnp.concatenate([jnp.where(causal, z_0[:t], NEG_BIG),
                               jnp.where(causal, z_b_diag, NEG_BIG)], axis=0)
    z_second = jnp.concatenate([jnp.where(qi > 0, z_a_left, NEG_BIG), z_0[t:]], axis=0)

    def suffix_sums(z):
        sp = _softplus2(z)
        hi = sp.astype(BF16)
        lo = (sp - hi.astype(F32)).astype(BF16)
        return jnp.dot(jnp.concatenate([hi, lo], axis=1), u_ref[...], preferred_element_type=F32)

    cum_first = suffix_sums(z_first)
    cum_second = suffix_sums(z_second)
    tot_first = jnp.broadcast_to(cum_first[:, 0:1], (2 * t, LANES))
    carry = tot_first + jnp.broadcast_to(cum_second[:, 0:1], (2 * t, LANES))
    a_first = jnp.exp2(z_first - cum_first).astype(BF16)
    a_second = jnp.exp2(z_second - cum_second
                        - jnp.concatenate([tot_first] * (w // LANES), axis=1)).astype(BF16)
    v_0 = block(v_ref, b0)
    acc_a = (jnp.dot(a_first[:t], v_0, preferred_element_type=F32)
             + jnp.dot(a_second[:t], block(v_ref, bm), preferred_element_type=F32))
    acc_b = (jnp.dot(a_first[t:], block(v_ref, b0 + 1), preferred_element_type=F32)
             + jnp.dot(a_second[t:], v_0, preferred_element_type=F32))
    lane = lax.broadcasted_iota(jnp.int32, (w, LANES), 1)
    o_ref[0] = jnp.concatenate([jnp.where(lane < HEAD_DIM, acc_a[:w], acc_a[w:]),
                                jnp.where(lane < HEAD_DIM, acc_b[:w], acc_b[w:])], axis=0)
    return jnp.logical_or(qi == 0, jnp.min(carry) >= ZERO_WEIGHT_CARRY)


def _sb_walk(qi, q2, row, col, top, k_ref, v_ref, u_ref, o_ref, acc_ref, carry_ref, z_ref, hilo_ref,
             arg_ref, *, t, w):
    rows = 2 * t
    acc_ref[...] = jnp.zeros_like(acc_ref)
    carry_ref[...] = jnp.zeros_like(carry_ref)

    def scores(kb, masked, dst):
        kt = k_ref[0, pl.ds(pl.multiple_of(kb * w, w), w), :]
        z = lax.dot_general(q2, kt, _NT, preferred_element_type=F32)
        if masked:
            z = jnp.where(col + (kb * w - qi * t) < row, z, NEG_BIG)
        sp = _softplus2(z)
        hi = sp.astype(BF16)
        lo = (sp - hi.astype(F32)).astype(BF16)
        hilo_ref[dst] = jnp.concatenate([hi, lo], axis=1)
        z_ref[dst] = z

    def sums(src, dst):
        cum = jnp.dot(hilo_ref[src], u_ref[...], preferred_element_type=F32)
        c = carry_ref[...]
        arg_ref[dst] = z_ref[src] - cum - jnp.concatenate([c] * (w // LANES), axis=1)
        c = c + jnp.broadcast_to(cum[:, 0:1], (rows, LANES))
        carry_ref[...] = c
        return jnp.min(c)

    def values(kb, src):
        vt = v_ref[0, pl.ds(pl.multiple_of(kb * w, w), w), :]
        acc_ref[...] += jnp.dot(jnp.exp2(arg_ref[src]).astype(BF16), vt, preferred_element_type=F32)

    scores(top, True, 0)
    min_carry = sums(0, 0)
    scores(top - 1, True, 1)

    def live(state):
        p, min_carry = state
        return jnp.logical_and(p < (top - 1) // 2, min_carry < ZERO_WEIGHT_CARRY)

    def pair(state):
        p, _ = state
        kb = top - 2 * p
        scores(kb - 2, False, 0)
        values(kb, 0)
        sums(1, 1)
        scores(kb - 3, False, 1)
        values(kb - 1, 1)
        return p + 1, sums(0, 0)

    p, min_carry = lax.while_loop(live, pair, (jnp.int32(0), min_carry))
    kb = top - 2 * p
    values(kb, 0)

    @pl.when(min_carry < ZERO_WEIGHT_CARRY)
    def _():
        sums(1, 1)
        values(kb - 1, 1)

    lane = lax.broadcasted_iota(jnp.int32, (t, LANES), 1)
    o_ref[0] = jnp.where(lane < HEAD_DIM, acc_ref[:t], acc_ref[t:])


def _sb_attn(qkv, u2, t, w):
    b, s, _ = qkv.shape
    rows = 2 * t
    return pl.pallas_call(
        functools.partial(_sb_kernel, t=t, w=w),
        grid=(b, SB_HEADS // 2, s // t),
        in_specs=[
            pl.BlockSpec((1, t, LANES), lambda bi, p, qi: (bi, qi, SB_Q_BLK + p)),
            pl.BlockSpec((1, s, LANES), lambda bi, p, qi: (bi, 0, SB_K_BLK + p)),
            pl.BlockSpec((1, s, LANES), lambda bi, p, qi: (bi, 0, SB_V_BLK + p)),
            pl.BlockSpec((2 * w, w), lambda bi, p, qi: (0, 0)),
        ],
        out_specs=pl.BlockSpec((1, t, LANES), lambda bi, p, qi: (bi, qi, p)),
        out_shape=jax.ShapeDtypeStruct((b, s, SB_WIDTH), F32),
        scratch_shapes=[
            pltpu.VMEM((rows, LANES), F32),
            pltpu.VMEM((rows, LANES), F32),
            pltpu.VMEM((2, rows, w), F32),
            pltpu.VMEM((2, rows, 2 * w), BF16),
            pltpu.VMEM((2, rows, w), F32),
        ],
        compiler_params=pltpu.CompilerParams(
            dimension_semantics=("arbitrary", "arbitrary", "arbitrary"),
            vmem_limit_bytes=VMEM_LIMIT_BYTES),
        name="sb_attn",
    )(qkv, qkv, qkv, u2)


def _diff_kernel(q_ref, k_ref, v_ref, lam_ref, slope_ref, g_ref, o_ref, acc_ref, m_ref, alpha_ref,
                 mtile_ref, s_ref, knorm_ref, *, t, out_scale):
    hd = pl.program_id(1)
    qi = pl.program_id(2)
    rows = 2 * t
    reps = t // LANES
    q2 = _stack_halves(q_ref[0])
    slope2 = slope_ref[hd]
    key_iota = lax.broadcasted_iota(jnp.int32, (1, t), 1)
    ones = jnp.ones((t, LANES), BF16)

    acc_ref[...] = jnp.zeros_like(acc_ref)
    m_ref[...] = jnp.full_like(m_ref, NEG_BIG)

    def qk(kb):
        kt = k_ref[0, pl.ds(pl.multiple_of(kb * t, t), t), :]
        return lax.dot_general(q2, kt, _NT, preferred_element_type=F32)

    def scores(kb, valid, dst):
        bias = (key_iota + (kb - qi) * t).astype(F32) * slope2
        s2 = qk(kb) + bias
        if valid is not None:
            s2 = jnp.where(valid, s2, NEG_BIG)
        m_old = m_ref[...]
        m_new = jnp.maximum(m_old, jnp.max(s2, axis=-1, keepdims=True))
        s_ref[dst] = s2
        mtile_ref[dst] = m_new
        alpha_ref[dst] = jnp.exp2(m_old - m_new)
        m_ref[...] = m_new

    def min_running_max():
        m = m_ref[...]
        return jnp.min(m[:t]), jnp.min(m[t:])

    def values(kb, src):
        vt = v_ref[0, pl.ds(pl.multiple_of(kb * t, t), t), :]
        alpha = alpha_ref[src]
        p = jnp.exp2(s_ref[src] - jnp.concatenate([mtile_ref[src]] * reps, axis=1))
        acc_ref[...] = (acc_ref[...] * jnp.concatenate([alpha, alpha], axis=1)
                        + jnp.dot(p.astype(BF16), jnp.concatenate([vt, ones], axis=1),
                                  preferred_element_type=F32))

    lane = lax.broadcasted_iota(jnp.int32, (t, LANES), 1)
    lo_half = lane < HEAD_DIM

    @pl.when(qi == 0)
    def _():
        def tile_norms(j, best):
            kt = k_ref[0, pl.ds(pl.multiple_of(j * t, t), t), :].astype(F32)
            sq = kt * kt
            n1 = jnp.max(jnp.sum(jnp.where(lo_half, sq, 0.0), axis=-1, keepdims=True))
            n2 = jnp.max(jnp.sum(jnp.where(lo_half, 0.0, sq), axis=-1, keepdims=True))
            best = (jnp.maximum(best[0], n1), jnp.maximum(best[1], n2))
            knorm_ref[0, j] = best[0]
            knorm_ref[1, j] = best[1]
            return best

        lax.fori_loop(0, k_ref.shape[1] // t, tile_norms, (jnp.zeros((), F32), jnp.zeros((), F32)))

    q_sq = q2.astype(F32)
    q_sq = jnp.sum(q_sq * q_sq, axis=-1, keepdims=True)
    qn1 = jnp.max(q_sq[:t])
    qn2 = jnp.max(q_sq[t:])

    def all_zero(kb, mmin1, mmin2):
        bias_max = ((kb - qi) * t + (t - 1)).astype(F32) * slope2

        def below(qn, kn, mmin):
            room = mmin - ZERO_WEIGHT_MARGIN - bias_max
            return jnp.logical_and(room > 0.0, qn * kn <= room * room)

        return jnp.logical_and(below(qn1, knorm_ref[0, kb], mmin1), below(qn2, knorm_ref[1, kb], mmin2))

    row = lax.broadcasted_iota(jnp.int32, (rows, t), 0) & (t - 1)
    col = lax.broadcasted_iota(jnp.int32, (rows, t), 1)
    scores(qi, col <= row, 0)

    def needed(j, mmin):
        kb = jnp.maximum(qi - j, 0)
        return jnp.logical_and(j <= qi, jnp.logical_not(all_zero(kb, *mmin)))

    def live(state):
        return needed(state[0] + 2, state[1:])

    def pair(state):
        j = state[0]
        mmin = min_running_max()
        scores(qi - j - 1, None, 1)
        values(qi - j, 0)
        scores(qi - j - 2, None, 0)
        values(qi - j - 1, 1)
        return (j + 2,) + mmin

    state = lax.while_loop(live, pair, (jnp.int32(0),) + min_running_max())
    j = state[0]
    one_more = needed(j + 1, min_running_max())

    @pl.when(one_more)
    def _():
        scores(qi - j - 1, None, 1)
        values(qi - j, 0)
        values(qi - j - 1, 1)

    @pl.when(jnp.logical_not(one_more))
    def _():
        values(qi - j, 0)

    a1 = acc_ref[:t]
    a2 = acc_ref[t:]
    o = a1[:, :LANES] / a1[:, LANES:] - lam_ref[0, 0] * (a2[:, :LANES] / a2[:, LANES:])
    o_ref[0] = _rms_norm(o, g_ref[...]) * out_scale


def _diff_attn(qkv, lam, subln_g, t, out_scale):
    b, s, _ = qkv.shape
    rows = 2 * t
    slopes2 = jnp.asarray([2.0 ** (-8.0 * (h + 1) / DIFF_HEADS) * LOG2E for h in range(DIFF_HEADS)], F32)
    return pl.pallas_call(
        functools.partial(_diff_kernel, t=t, out_scale=out_scale),
        grid=(b, DIFF_HEADS, s // t),
        in_specs=[
            pl.BlockSpec((1, t, LANES), lambda bi, h, qi: (bi, qi, DF_Q_BLK + h)),
            pl.BlockSpec((1, s, LANES), lambda bi, h, qi: (bi, 0, DF_K_BLK + h)),
            pl.BlockSpec((1, s, LANES), lambda bi, h, qi: (bi, 0, DF_V_BLK + h)),
            pl.BlockSpec(memory_space=pltpu.SMEM),
            pl.BlockSpec(memory_space=pltpu.SMEM),
            pl.BlockSpec((1, LANES), lambda bi, h, qi: (0, 0)),
        ],
        out_specs=pl.BlockSpec((1, t, LANES), lambda bi, h, qi: (bi, qi, h)),
        out_shape=jax.ShapeDtypeStruct((b, s, DIFF_WIDTH), F32),
        scratch_shapes=[
            pltpu.VMEM((rows, 2 * LANES), F32),
            pltpu.VMEM((rows, LANES), F32),
            pltpu.VMEM((2, rows, LANES), F32),
            pltpu.VMEM((2, rows, LANES), F32),
            pltpu.VMEM((2, rows, t), F32),
            pltpu.SMEM((2, s // t), F32),
        ],
        compiler_params=pltpu.CompilerParams(
            dimension_semantics=("arbitrary", "arbitrary", "arbitrary"),
            vmem_limit_bytes=VMEM_LIMIT_BYTES),
        name="diff_attn",
    )(qkv, qkv, qkv, lam, slopes2, subln_g)


def _lambda_kernel(q1_ref, k1_ref, q2_ref, k2_ref, o_ref, *, lambda_init):
    s1 = jnp.sum(q1_ref[...] * k1_ref[...], axis=-1, keepdims=True)
    s2 = jnp.sum(q2_ref[...] * k2_ref[...], axis=-1, keepdims=True)
    o_ref[...] = jnp.exp(s1) - jnp.exp(s2) + lambda_init


def _lambda(q1, k1, q2, k2, lambda_init):
    return pl.pallas_call(
        functools.partial(_lambda_kernel, lambda_init=lambda_init),
        out_shape=jax.ShapeDtypeStruct((1, 1), F32),
        name="diff_lambda",
    )(q1, k1, q2, k2)


def _out_kernel(x_ref, sb_ref, df_ref, g_pre_ref, w_zg_ref, b_ref, w_sb_ref, w_df_ref, w_out_ref,
                g_post_ref, o_ref):
    x = x_ref[...]
    h = _rms_norm(x, g_pre_ref[...]).astype(BF16)
    zg = jnp.dot(h, w_zg_ref[...], preferred_element_type=F32)
    d = x.shape[-1]
    sb_z = zg[:, :SB_WIDTH]
    df_z = zg[:, SB_WIDTH:SB_WIDTH + DIFF_WIDTH]
    gates = jax.nn.sigmoid(zg[:, SB_WIDTH + DIFF_WIDTH:] + b_ref[...])
    y_sb = jnp.dot((sb_ref[...] * (sb_z * jax.nn.sigmoid(sb_z))).astype(BF16), w_sb_ref[...],
                   preferred_element_type=F32)
    y_df = jnp.dot((df_ref[...] * (df_z * jax.nn.sigmoid(df_z))).astype(BF16), w_df_ref[...],
                   preferred_element_type=F32)
    merged = gates[:, :d] * y_sb + gates[:, d:] * y_df
    out = jnp.dot(merged.astype(BF16), w_out_ref[...], preferred_element_type=F32)
    o_ref[...] = x + _rms_norm(out, g_post_ref[...])


def _out_proj(x2d, sb, df, g_pre, w_zg, b_gate, w_sb, w_df, w_out, g_post, tm):
    n, d = x2d.shape
    zg_cols = w_zg.shape[1]
    full = lambda shape: pl.BlockSpec(shape, lambda i: (0, 0))
    return pl.pallas_call(
        _out_kernel,
        grid=(n // tm,),
        in_specs=[
            pl.BlockSpec((tm, d), lambda i: (i, 0)),
            pl.BlockSpec((tm, SB_WIDTH), lambda i: (i, 0)),
            pl.BlockSpec((tm, DIFF_WIDTH), lambda i: (i, 0)),
            full((1, d)),
            full((d, zg_cols)),
            full((1, 2 * d)),
            full((SB_WIDTH, d)),
            full((DIFF_WIDTH, d)),
            full((d, d)),
            full((1, d)),
        ],
        out_specs=pl.BlockSpec((tm, d), lambda i: (i, 0)),
        out_shape=jax.ShapeDtypeStruct((n, d), F32),
        compiler_params=pltpu.CompilerParams(
            dimension_semantics=("arbitrary",), vmem_limit_bytes=VMEM_LIMIT_BYTES),
        name="out_proj",
    )(x2d, sb, df, g_pre, w_zg, b_gate, w_sb, w_df, w_out, g_post)


def _split_w_in(w):
    sizes = (SB_WIDTH,) * 4 + (DIFF_WIDTH,) * 4 + (w.shape[0],) * 2
    parts, off = [], 0
    for n in sizes:
        parts.append(w[:, off:off + n])
        off += n
    return parts


def kernel(x, pre_norm_g, w_in, b_gate, lambda_q1, lambda_k1, lambda_q2, lambda_k2, subln_g, w_o_sb,
           w_o_diff, w_out, post_norm_g):
    b, s, d = x.shape
    depth = w_in.shape[0]
    t, w = ATTN_T, SUM_W
    assert s % t == 0 and s >= 2 * t and t % w == 0
    tm = min(256, b * s)
    u = (lax.broadcasted_iota(jnp.int32, (w, w), 0)
         >= lax.broadcasted_iota(jnp.int32, (w, w), 1)).astype(BF16)
    u2 = jnp.concatenate([u, u], axis=0)
    q_scale = jnp.full((1, SB_WIDTH), QK_SCALE * LOG2E, F32)
    one = jnp.ones((1, SB_WIDTH), F32)
    col_scale = jnp.concatenate([q_scale, one, one, q_scale, one, one], axis=1)
    for layer in range(depth):
        sb_q, sb_k, sb_v, sb_z, df_q, df_k, df_v, df_z, g_sb, g_df = _split_w_in(w_in[layer])
        w_qkv = jnp.concatenate([sb_q, sb_k, sb_v, df_q, df_k, df_v], axis=1).astype(BF16)
        w_zg = jnp.concatenate([sb_z, df_z, g_sb, g_df], axis=1).astype(BF16)
        lambda_init = 0.8 - 0.6 * math.exp(-0.3 * layer)
        x2d = x.reshape(b * s, d)
        g_pre = pre_norm_g[layer].reshape(1, d)

        qkv = _qkv_proj(x2d, g_pre, w_qkv, col_scale, tm).reshape(b, s, QKV_COLS)
        sb_out = _sb_attn(qkv, u2, t, w)
        lam = _lambda(lambda_q1[layer][None], lambda_k1[layer][None], lambda_q2[layer][None],
                      lambda_k2[layer][None], lambda_init)
        df_out = _diff_attn(qkv, lam, subln_g[layer].reshape(1, LANES), t, 1.0 - lambda_init)
        x = _out_proj(x2d, sb_out.reshape(b * s, SB_WIDTH), df_out.reshape(b * s, DIFF_WIDTH), g_pre,
                      w_zg, b_gate[layer].reshape(1, 2 * d), w_o_sb[layer].astype(BF16),
                      w_o_diff[layer].astype(BF16), w_out[layer].astype(BF16),
                      post_norm_g[layer].reshape(1, d), tm).reshape(b, s, d)
    return x
```

```python
import functools
import math

import jax
import jax.numpy as jnp
from jax import lax
from jax.experimental import pallas as pl
from jax.experimental.pallas import tpu as pltpu

F32 = jnp.float32
BF16 = jnp.bfloat16

HEAD_DIM = 64
SB_HEADS = 8
DIFF_HEADS = 4
SB_WIDTH = SB_HEADS * HEAD_DIM
DIFF_WIDTH = DIFF_HEADS * 2 * HEAD_DIM
NORM_EPS = 1e-6
LANES = 128
MXU_DIM = 256
LOG2E = 1.4426950408889634
QK_SCALE = HEAD_DIM ** -0.5
NEG_BIG = -1e30
SOFTPLUS_CLAMP = 100.0
ZERO_WEIGHT_CARRY = 152.0
ZERO_WEIGHT_MARGIN = 156.0
VMEM_LIMIT_BYTES = 56 * 1024 * 1024

SUM_W = MXU_DIM
ATTN_T = 2 * SUM_W
SB_TILES_PER_STEP = 2

QKV_COLS = 3 * SB_WIDTH + 3 * DIFF_WIDTH
SB_Q_BLK, SB_K_BLK, SB_V_BLK = 0, 4, 8
DF_Q_BLK, DF_K_BLK, DF_V_BLK = 12, 16, 20

_NT = (((1,), (1,)), ((), ()))


def _rms_norm(x32, g32):
    return x32 * lax.rsqrt(jnp.mean(x32 * x32, axis=-1, keepdims=True) + NORM_EPS) * g32


def _stack_halves(q):
    q32 = q.astype(F32)
    lane = lax.broadcasted_iota(jnp.int32, q.shape, 1)
    return jnp.concatenate([jnp.where(lane < HEAD_DIM, q32, 0.0).astype(BF16),
                            jnp.where(lane >= HEAD_DIM, q32, 0.0).astype(BF16)], axis=0)


def _qkv_kernel(x_ref, g_ref, w_ref, cs_ref, o_ref):
    h = _rms_norm(x_ref[...], g_ref[...]).astype(BF16)
    p = jnp.dot(h, w_ref[...], preferred_element_type=F32)
    o_ref[...] = (p * cs_ref[...]).astype(BF16)


def _qkv_proj(x2d, g, w_qkv, col_scale, tm):
    n, d = x2d.shape
    return pl.pallas_call(
        _qkv_kernel,
        grid=(n // tm,),
        in_specs=[
            pl.BlockSpec((tm, d), lambda i: (i, 0)),
            pl.BlockSpec((1, d), lambda i: (0, 0)),
            pl.BlockSpec((d, QKV_COLS), lambda i: (0, 0)),
            pl.BlockSpec((1, QKV_COLS), lambda i: (0, 0)),
        ],
        out_specs=pl.BlockSpec((tm, QKV_COLS), lambda i: (i, 0)),
        out_shape=jax.ShapeDtypeStruct((n, QKV_COLS), BF16),
        compiler_params=pltpu.CompilerParams(
            dimension_semantics=("arbitrary",), vmem_limit_bytes=VMEM_LIMIT_BYTES),
        name="qkv_proj",
    )(x2d, g, w_qkv, col_scale)


def _sb_kernel(q_ref, k_ref, v_ref, u_ref, o_ref, acc_ref, carry_ref, z_ref, hilo_ref, arg_ref, *,
               t, w):
    rows = 2 * t
    row = lax.broadcasted_iota(jnp.int32, (rows, w), 0) & (t - 1)
    col = lax.broadcasted_iota(jnp.int32, (rows, w), 1)
    n_sub = q_ref.shape[1] // t

    tiles = []
    for sub in range(n_sub):
        qi = pl.program_id(2) * n_sub + sub
        q2 = _stack_halves(q_ref[0, sub * t:(sub + 1) * t, :])
        o_tile = o_ref.at[0, sub * t:(sub + 1) * t, :]
        tiles.append((qi, q2, o_tile, _sb_band(qi, q2, k_ref, v_ref, u_ref, o_tile, w=w)))

    for qi, q2, o_tile, done in tiles:
        @pl.when(jnp.logical_not(done))
        def _(qi=qi, q2=q2, o_tile=o_tile):
            top = (qi + 1) * (t // w) - 1
            _sb_walk(qi, q2, row, col, top, k_ref, v_ref, u_ref, o_tile, acc_ref, carry_ref, z_ref,
                     hilo_ref, arg_ref, t=t, w=w)


def _softplus2(z):
    return jnp.maximum(z, jnp.log(1.0 + jnp.exp2(jnp.minimum(z, SOFTPLUS_CLAMP))) * LOG2E)


def _sb_band(qi, q2, k_ref, v_ref, u_ref, o_ref, *, w):
    t = 2 * w

    def block(ref, kb):
        return ref[0, pl.ds(pl.multiple_of(kb * w, w), w), :]

    b0 = 2 * qi
    bm = jnp.maximum(b0 - 1, 0)
    q_a = jnp.concatenate([q2[0:w], q2[t:t + w]], axis=0)
    q_b = jnp.concatenate([q2[w:t], q2[t + w:2 * t]], axis=0)
    z_0 = lax.dot_general(jnp.concatenate([q_a, q_b], axis=0), block(k_ref, b0), _NT,
                          preferred_element_type=F32)
    z_b_diag = lax.dot_general(q_b, block(k_ref, b0 + 1), _NT, preferred_element_type=F32)
    z_a_left = lax.dot_general(q_a, block(k_ref, bm), _NT, preferred_element_type=F32)
    causal = (lax.broadcasted_iota(jnp.int32, (t, w), 1)
              < (lax.broadcasted_iota(jnp.int32, (t, w), 0) & (w - 1)))
    z_first = jnp.concatenate([jnp.where(causal, z_0[:t], NEG_BIG),
                               jnp.where(causal, z_b_diag, NEG_BIG)], axis=0)
    z_second = jnp.concatenate([jnp.where(qi > 0, z_a_left, NEG_BIG), z_0[t:]], axis=0)

    def suffix_sums(z):
        sp = _softplus2(z)
        hi = sp.astype(BF16)
        lo = (sp - hi.astype(F32)).astype(BF16)
        return jnp.dot(jnp.concatenate([hi, lo], axis=1), u_ref[...], preferred_element_type=F32)

    cum_first = suffix_sums(z_first)
    cum_second = suffix_sums(z_second)
    tot_first = jnp.broadcast_to(cum_first[:, 0:1], (2 * t, LANES))
    carry = tot_first + jnp.broadcast_to(cum_second[:, 0:1], (2 * t, LANES))
    a_first = jnp.exp2(z_first - cum_first).astype(BF16)
    a_second = jnp.exp2(z_second - cum_second
                        - jnp.concatenate([tot_first] * (w // LANES), axis=1)).astype(BF16)
    v_0 = block(v_ref, b0)
    acc_a = (jnp.dot(a_first[:t], v_0, preferred_element_type=F32)
             + jnp.dot(a_second[:t], block(v_ref, bm), preferred_element_type=F32))
    acc_b = (jnp.dot(a_first[t:], block(v_ref, b0 + 1), preferred_element_type=F32)
             + jnp.dot(a_second[t:], v_0, preferred_element_type=F32))
    lane = lax.broadcasted_iota(jnp.int32, (w, LANES), 1)
    o_ref[...] = jnp.concatenate([jnp.where(lane < HEAD_DIM, acc_a[:w], acc_a[w:]),
                                  jnp.where(lane < HEAD_DIM, acc_b[:w], acc_b[w:])], axis=0)
    return jnp.logical_or(qi == 0, jnp.min(carry) >= ZERO_WEIGHT_CARRY)


def _sb_walk(qi, q2, row, col, top, k_ref, v_ref, u_ref, o_ref, acc_ref, carry_ref, z_ref, hilo_ref,
             arg_ref, *, t, w):
    rows = 2 * t
    acc_ref[...] = jnp.zeros_like(acc_ref)
    carry_ref[...] = jnp.zeros_like(carry_ref)

    def scores(kb, masked, dst):
        kt = k_ref[0, pl.ds(pl.multiple_of(kb * w, w), w), :]
        z = lax.dot_general(q2, kt, _NT, preferred_element_type=F32)
        if masked:
            z = jnp.where(col + (kb * w - qi * t) < row, z, NEG_BIG)
        sp = _softplus2(z)
        hi = sp.astype(BF16)
        lo = (sp - hi.astype(F32)).astype(BF16)
        hilo_ref[dst] = jnp.concatenate([hi, lo], axis=1)
        z_ref[dst] = z

    def sums(src, dst):
        cum = jnp.dot(hilo_ref[src], u_ref[...], preferred_element_type=F32)
        c = carry_ref[...]
        arg_ref[dst] = z_ref[src] - cum - jnp.concatenate([c] * (w // LANES), axis=1)
        c = c + jnp.broadcast_to(cum[:, 0:1], (rows, LANES))
        carry_ref[...] = c
        return jnp.min(c)

    def values(kb, src):
        vt = v_ref[0, pl.ds(pl.multiple_of(kb * w, w), w), :]
        acc_ref[...] += jnp.dot(jnp.exp2(arg_ref[src]).astype(BF16), vt, preferred_element_type=F32)

    scores(top, True, 0)
    min_carry = sums(0, 0)
    scores(top - 1, True, 1)

    def live(state):
        p, min_carry = state
        return jnp.logical_and(p < (top - 1) // 2, min_carry < ZERO_WEIGHT_CARRY)

    def pair(state):
        p, _ = state
        kb = top - 2 * p
        scores(kb - 2, False, 0)
        values(kb, 0)
        sums(1, 1)
        scores(kb - 3, False, 1)
        values(kb - 1, 1)
        return p + 1, sums(0, 0)

    p, min_carry = lax.while_loop(live, pair, (jnp.int32(0), min_carry))
    kb = top - 2 * p
    values(kb, 0)

    @pl.when(min_carry < ZERO_WEIGHT_CARRY)
    def _():
        sums(1, 1)
        values(kb - 1, 1)

    lane = lax.broadcasted_iota(jnp.int32, (t, LANES), 1)
    o_ref[...] = jnp.where(lane < HEAD_DIM, acc_ref[:t], acc_ref[t:])


def _sb_attn(qkv, u2, t, w, tiles_per_step):
    b, s, _ = qkv.shape
    rows = 2 * t
    tq = tiles_per_step * t
    return pl.pallas_call(
        functools.partial(_sb_kernel, t=t, w=w),
        grid=(b, SB_HEADS // 2, s // tq),
        in_specs=[
            pl.BlockSpec((1, tq, LANES), lambda bi, p, qi: (bi, qi, SB_Q_BLK + p)),
            pl.BlockSpec((1, s, LANES), lambda bi, p, qi: (bi, 0, SB_K_BLK + p)),
            pl.BlockSpec((1, s, LANES), lambda bi, p, qi: (bi, 0, SB_V_BLK + p)),
            pl.BlockSpec((2 * w, w), lambda bi, p, qi: (0, 0)),
        ],
        out_specs=pl.BlockSpec((1, tq, LANES), lambda bi, p, qi: (bi, qi, p)),
        out_shape=jax.ShapeDtypeStruct((b, s, SB_WIDTH), F32),
        scratch_shapes=[
            pltpu.VMEM((rows, LANES), F32),
            pltpu.VMEM((rows, LANES), F32),
            pltpu.VMEM((2, rows, w), F32),
            pltpu.VMEM((2, rows, 2 * w), BF16),
            pltpu.VMEM((2, rows, w), F32),
        ],
        compiler_params=pltpu.CompilerParams(
            dimension_semantics=("arbitrary", "arbitrary", "arbitrary"),
            vmem_limit_bytes=VMEM_LIMIT_BYTES),
        name="sb_attn",
    )(qkv, qkv, qkv, u2)


def _diff_kernel(q_ref, k_ref, v_ref, lam_ref, slope_ref, g_ref, o_ref, acc_ref, m_ref, alpha_ref,
                 mtile_ref, s_ref, knorm_ref, *, t, out_scale):
    hd = pl.program_id(1)
    qi = pl.program_id(2)
    rows = 2 * t
    reps = t // LANES
    q2 = _stack_halves(q_ref[0])
    slope2 = slope_ref[hd]
    key_iota = lax.broadcasted_iota(jnp.int32, (1, t), 1)
    ones = jnp.ones((t, LANES), BF16)

    acc_ref[...] = jnp.zeros_like(acc_ref)
    m_ref[...] = jnp.full_like(m_ref, NEG_BIG)

    def qk(kb):
        kt = k_ref[0, pl.ds(pl.multiple_of(kb * t, t), t), :]
        return lax.dot_general(q2, kt, _NT, preferred_element_type=F32)

    def scores(kb, valid, dst):
        bias = (key_iota + (kb - qi) * t).astype(F32) * slope2
        s2 = qk(kb) + bias
        if valid is not None:
            s2 = jnp.where(valid, s2, NEG_BIG)
        m_old = m_ref[...]
        m_new = jnp.maximum(m_old, jnp.max(s2, axis=-1, keepdims=True))
        s_ref[dst] = s2
        mtile_ref[dst] = m_new
        alpha_ref[dst] = jnp.exp2(m_old - m_new)
        m_ref[...] = m_new

    def min_running_max():
        m = m_ref[...]
        return jnp.min(m[:t]), jnp.min(m[t:])

    def values(kb, src):
        vt = v_ref[0, pl.ds(pl.multiple_of(kb * t, t), t), :]
        alpha = alpha_ref[src]
        p = jnp.exp2(s_ref[src] - jnp.concatenate([mtile_ref[src]] * reps, axis=1))
        acc_ref[...] = (acc_ref[...] * jnp.concatenate([alpha, alpha], axis=1)
                        + jnp.dot(p.astype(BF16), jnp.concatenate([vt, ones], axis=1),
                                  preferred_element_type=F32))

    lane = lax.broadcasted_iota(jnp.int32, (t, LANES), 1)
    lo_half = lane < HEAD_DIM

    @pl.when(qi == 0)
    def _():
        def tile_norms(j, best):
            kt = k_ref[0, pl.ds(pl.multiple_of(j * t, t), t), :].astype(F32)
            sq = kt * kt
            n1 = jnp.max(jnp.sum(jnp.where(lo_half, sq, 0.0), axis=-1, keepdims=True))
            n2 = jnp.max(jnp.sum(jnp.where(lo_half, 0.0, sq), axis=-1, keepdims=True))
            best = (jnp.maximum(best[0], n1), jnp.maximum(best[1], n2))
            knorm_ref[0, j] = best[0]
            knorm_ref[1, j] = best[1]
            return best

        lax.fori_loop(0, k_ref.shape[1] // t, tile_norms, (jnp.zeros((), F32), jnp.zeros((), F32)))

    q_sq = q2.astype(F32)
    q_sq = jnp.sum(q_sq * q_sq, axis=-1, keepdims=True)
    qn1 = jnp.max(q_sq[:t])
    qn2 = jnp.max(q_sq[t:])

    def all_zero(kb, mmin1, mmin2):
        bias_max = ((kb - qi) * t + (t - 1)).astype(F32) * slope2

        def below(qn, kn, mmin):
            room = mmin - ZERO_WEIGHT_MARGIN - bias_max
            return jnp.logical_and(room > 0.0, qn * kn <= room * room)

        return jnp.logical_and(below(qn1, knorm_ref[0, kb], mmin1), below(qn2, knorm_ref[1, kb], mmin2))

    row = lax.broadcasted_iota(jnp.int32, (rows, t), 0) & (t - 1)
    col = lax.broadcasted_iota(jnp.int32, (rows, t), 1)
    scores(qi, col <= row, 0)

    def needed(j, mmin):
        kb = jnp.maximum(qi - j, 0)
        return jnp.logical_and(j <= qi, jnp.logical_not(all_zero(kb, *mmin)))

    def live(state):
        return needed(state[0] + 2, state[1:])

    def pair(state):
        j = state[0]
        mmin = min_running_max()
        scores(qi - j - 1, None, 1)
        values(qi - j, 0)
        scores(qi - j - 2, None, 0)
        values(qi - j - 1, 1)
        return (j + 2,) + mmin

    state = lax.while_loop(live, pair, (jnp.int32(0),) + min_running_max())
    j = state[0]
    one_more = needed(j + 1, min_running_max())

    @pl.when(one_more)
    def _():
        scores(qi - j - 1, None, 1)
        values(qi - j, 0)
        values(qi - j - 1, 1)

    @pl.when(jnp.logical_not(one_more))
    def _():
        values(qi - j, 0)

    a1 = acc_ref[:t]
    a2 = acc_ref[t:]
    o = a1[:, :LANES] / a1[:, LANES:] - lam_ref[0, 0] * (a2[:, :LANES] / a2[:, LANES:])
    o_ref[0] = _rms_norm(o, g_ref[...]) * out_scale


def _diff_attn(qkv, lam, subln_g, t, out_scale):
    b, s, _ = qkv.shape
    rows = 2 * t
    slopes2 = jnp.asarray([2.0 ** (-8.0 * (h + 1) / DIFF_HEADS) * LOG2E for h in range(DIFF_HEADS)], F32)
    return pl.pallas_call(
        functools.partial(_diff_kernel, t=t, out_scale=out_scale),
        grid=(b, DIFF_HEADS, s // t),
        in_specs=[
            pl.BlockSpec((1, t, LANES), lambda bi, h, qi: (bi, qi, DF_Q_BLK + h)),
            pl.BlockSpec((1, s, LANES), lambda bi, h, qi: (bi, 0, DF_K_BLK + h)),
            pl.BlockSpec((1, s, LANES), lambda bi, h, qi: (bi, 0, DF_V_BLK + h)),
            pl.BlockSpec(memory_space=pltpu.SMEM),
            pl.BlockSpec(memory_space=pltpu.SMEM),
            pl.BlockSpec((1, LANES), lambda bi, h, qi: (0, 0)),
        ],
        out_specs=pl.BlockSpec((1, t, LANES), lambda bi, h, qi: (bi, qi, h)),
        out_shape=jax.ShapeDtypeStruct((b, s, DIFF_WIDTH), F32),
        scratch_shapes=[
            pltpu.VMEM((rows, 2 * LANES), F32),
            pltpu.VMEM((rows, LANES), F32),
            pltpu.VMEM((2, rows, LANES), F32),
            pltpu.VMEM((2, rows, LANES), F32),
            pltpu.VMEM((2, rows, t), F32),
            pltpu.SMEM((2, s // t), F32),
        ],
        compiler_params=pltpu.CompilerParams(
            dimension_semantics=("arbitrary", "arbitrary", "arbitrary"),
            vmem_limit_bytes=VMEM_LIMIT_BYTES),
        name="diff_attn",
    )(qkv, qkv, qkv, lam, slopes2, subln_g)


def _lambda_kernel(q1_ref, k1_ref, q2_ref, k2_ref, o_ref, *, lambda_init):
    s1 = jnp.sum(q1_ref[...] * k1_ref[...], axis=-1, keepdims=True)
    s2 = jnp.sum(q2_ref[...] * k2_ref[...], axis=-1, keepdims=True)
    o_ref[...] = jnp.exp(s1) - jnp.exp(s2) + lambda_init


def _lambda(q1, k1, q2, k2, lambda_init):
    return pl.pallas_call(
        functools.partial(_lambda_kernel, lambda_init=lambda_init),
        out_shape=jax.ShapeDtypeStruct((1, 1), F32),
        name="diff_lambda",
    )(q1, k1, q2, k2)


def _out_kernel(x_ref, sb_ref, df_ref, g_pre_ref, w_zg_ref, b_ref, w_sb_ref, w_df_ref, w_out_ref,
                g_post_ref, o_ref):
    x = x_ref[...]
    h = _rms_norm(x, g_pre_ref[...]).astype(BF16)
    zg = jnp.dot(h, w_zg_ref[...], preferred_element_type=F32)
    d = x.shape[-1]
    sb_z = zg[:, :SB_WIDTH]
    df_z = zg[:, SB_WIDTH:SB_WIDTH + DIFF_WIDTH]
    gates = jax.nn.sigmoid(zg[:, SB_WIDTH + DIFF_WIDTH:] + b_ref[...])
    y_sb = jnp.dot((sb_ref[...] * (sb_z * jax.nn.sigmoid(sb_z))).astype(BF16), w_sb_ref[...],
                   preferred_element_type=F32)
    y_df = jnp.dot((df_ref[...] * (df_z * jax.nn.sigmoid(df_z))).astype(BF16), w_df_ref[...],
                   preferred_element_type=F32)
    merged = gates[:, :d] * y_sb + gates[:, d:] * y_df
    out = jnp.dot(merged.astype(BF16), w_out_ref[...], preferred_element_type=F32)
    o_ref[...] = x + _rms_norm(out, g_post_ref[...])


def _out_proj(x2d, sb, df, g_pre, w_zg, b_gate, w_sb, w_df, w_out, g_post, tm):
    n, d = x2d.shape
    zg_cols = w_zg.shape[1]
    full = lambda shape: pl.BlockSpec(shape, lambda i: (0, 0))
    return pl.pallas_call(
        _out_kernel,
        grid=(n // tm,),
        in_specs=[
            pl.BlockSpec((tm, d), lambda i: (i, 0)),
            pl.BlockSpec((tm, SB_WIDTH), lambda i: (i, 0)),
            pl.BlockSpec((tm, DIFF_WIDTH), lambda i: (i, 0)),
            full((1, d)),
            full((d, zg_cols)),
            full((1, 2 * d)),
            full((SB_WIDTH, d)),
            full((DIFF_WIDTH, d)),
            full((d, d)),
            full((1, d)),
        ],
        out_specs=pl.BlockSpec((tm, d), lambda i: (i, 0)),
        out_shape=jax.ShapeDtypeStruct((n, d), F32),
        compiler_params=pltpu.CompilerParams(
            dimension_semantics=("arbitrary",), vmem_limit_bytes=VMEM_LIMIT_BYTES),
        name="out_proj",
    )(x2d, sb, df, g_pre, w_zg, b_gate, w_sb, w_df, w_out, g_post)


def _split_w_in(w):
    sizes = (SB_WIDTH,) * 4 + (DIFF_WIDTH,) * 4 + (w.shape[0],) * 2
    parts, off = [], 0
    for n in sizes:
        parts.append(w[:, off:off + n])
        off += n
    return parts


def kernel(x, pre_norm_g, w_in, b_gate, lambda_q1, lambda_k1, lambda_q2, lambda_k2, subln_g, w_o_sb,
           w_o_diff, w_out, post_norm_g):
    b, s, d = x.shape
    depth = w_in.shape[0]
    t, w = ATTN_T, SUM_W
    assert s % t == 0 and s >= 2 * t and t % w == 0
    tm = min(512, b * s)
    u = (lax.broadcasted_iota(jnp.int32, (w, w), 0)
         >= lax.broadcasted_iota(jnp.int32, (w, w), 1)).astype(BF16)
    u2 = jnp.concatenate([u, u], axis=0)
    q_scale = jnp.full((1, SB_WIDTH), QK_SCALE * LOG2E, F32)
    one = jnp.ones((1, SB_WIDTH), F32)
    col_scale = jnp.concatenate([q_scale, one, one, q_scale, one, one], axis=1)
    for layer in range(depth):
        sb_q, sb_k, sb_v, sb_z, df_q, df_k, df_v, df_z, g_sb, g_df = _split_w_in(w_in[layer])
        w_qkv = jnp.concatenate([sb_q, sb_k, sb_v, df_q, df_k, df_v], axis=1).astype(BF16)
        w_zg = jnp.concatenate([sb_z, df_z, g_sb, g_df], axis=1).astype(BF16)
        lambda_init = 0.8 - 0.6 * math.exp(-0.3 * layer)
        x2d = x.reshape(b * s, d)
        g_pre = pre_norm_g[layer].reshape(1, d)

        qkv = _qkv_proj(x2d, g_pre, w_qkv, col_scale, tm).reshape(b, s, QKV_COLS)
        sb_out = _sb_attn(qkv, u2, t, w, SB_TILES_PER_STEP if s % (SB_TILES_PER_STEP * t) == 0 else 1)
        lam = _lambda(lambda_q1[layer][None], lambda_k1[layer][None], lambda_q2[layer][None],
                      lambda_k2[layer][None], lambda_init)
        df_out = _diff_attn(qkv, lam, subln_g[layer].reshape(1, LANES), t, 1.0 - lambda_init)
        x = _out_proj(x2d, sb_out.reshape(b * s, SB_WIDTH), df_out.reshape(b * s, DIFF_WIDTH), g_pre,
                      w_zg, b_gate[layer].reshape(1, 2 * d), w_o_sb[layer].astype(BF16),
                      w_o_diff[layer].astype(BF16), w_out[layer].astype(BF16),
                      post_norm_g[layer].reshape(1, d), tm).reshape(b, s, d)
    return x
```

```python
import functools
import math

import jax
import jax.numpy as jnp
from jax import lax
from jax.experimental import pallas as pl
from jax.experimental.pallas import tpu as pltpu

F32 = jnp.float32
BF16 = jnp.bfloat16

HEAD_DIM = 64
SB_HEADS = 8
DIFF_HEADS = 4
SB_WIDTH = SB_HEADS * HEAD_DIM
DIFF_WIDTH = DIFF_HEADS * 2 * HEAD_DIM
NORM_EPS = 1e-6
LANES = 128
MXU_DIM = 256
LOG2E = 1.4426950408889634
QK_SCALE = HEAD_DIM ** -0.5
NEG_BIG = -1e30
SOFTPLUS_CLAMP = 100.0
ZERO_WEIGHT_CARRY = 152.0
ZERO_WEIGHT_MARGIN = 156.0
VMEM_LIMIT_BYTES = 56 * 1024 * 1024

SUM_W = MXU_DIM
ATTN_T = 2 * SUM_W
SB_TILES_PER_STEP = 2
DIFF_TILES_PER_STEP = 2

QKV_COLS = 3 * SB_WIDTH + 3 * DIFF_WIDTH
SB_Q_BLK, SB_K_BLK, SB_V_BLK = 0, 4, 8
DF_Q_BLK, DF_K_BLK, DF_V_BLK = 12, 16, 20

_NT = (((1,), (1,)), ((), ()))


def _rms_norm(x32, g32):
    return x32 * lax.rsqrt(jnp.mean(x32 * x32, axis=-1, keepdims=True) + NORM_EPS) * g32


def _stack_halves(q):
    q32 = q.astype(F32)
    lane = lax.broadcasted_iota(jnp.int32, q.shape, 1)
    return jnp.concatenate([jnp.where(lane < HEAD_DIM, q32, 0.0).astype(BF16),
                            jnp.where(lane >= HEAD_DIM, q32, 0.0).astype(BF16)], axis=0)


def _qkv_kernel(x_ref, g_ref, w_ref, cs_ref, o_ref):
    h = _rms_norm(x_ref[...], g_ref[...]).astype(BF16)
    p = jnp.dot(h, w_ref[...], preferred_element_type=F32)
    o_ref[...] = (p * cs_ref[...]).astype(BF16)


def _qkv_proj(x2d, g, w_qkv, col_scale, tm):
    n, d = x2d.shape
    return pl.pallas_call(
        _qkv_kernel,
        grid=(n // tm,),
        in_specs=[
            pl.BlockSpec((tm, d), lambda i: (i, 0)),
            pl.BlockSpec((1, d), lambda i: (0, 0)),
            pl.BlockSpec((d, QKV_COLS), lambda i: (0, 0)),
            pl.BlockSpec((1, QKV_COLS), lambda i: (0, 0)),
        ],
        out_specs=pl.BlockSpec((tm, QKV_COLS), lambda i: (i, 0)),
        out_shape=jax.ShapeDtypeStruct((n, QKV_COLS), BF16),
        compiler_params=pltpu.CompilerParams(
            dimension_semantics=("arbitrary",), vmem_limit_bytes=VMEM_LIMIT_BYTES),
        name="qkv_proj",
    )(x2d, g, w_qkv, col_scale)


def _sb_kernel(q_ref, k_ref, v_ref, u_ref, o_ref, acc_ref, carry_ref, z_ref, hilo_ref, arg_ref, *,
               t, w):
    rows = 2 * t
    row = lax.broadcasted_iota(jnp.int32, (rows, w), 0) & (t - 1)
    col = lax.broadcasted_iota(jnp.int32, (rows, w), 1)
    n_sub = q_ref.shape[1] // t

    tiles = []
    for sub in range(n_sub):
        qi = pl.program_id(2) * n_sub + sub
        q2 = _stack_halves(q_ref[0, sub * t:(sub + 1) * t, :])
        o_tile = o_ref.at[0, sub * t:(sub + 1) * t, :]
        tiles.append((qi, q2, o_tile, _sb_band(qi, q2, k_ref, v_ref, u_ref, o_tile, w=w)))

    for qi, q2, o_tile, done in tiles:
        @pl.when(jnp.logical_not(done))
        def _(qi=qi, q2=q2, o_tile=o_tile):
            top = (qi + 1) * (t // w) - 1
            _sb_walk(qi, q2, row, col, top, k_ref, v_ref, u_ref, o_tile, acc_ref, carry_ref, z_ref,
                     hilo_ref, arg_ref, t=t, w=w)


def _softplus2(z):
    return jnp.maximum(z, jnp.log(1.0 + jnp.exp2(jnp.minimum(z, SOFTPLUS_CLAMP))) * LOG2E)


def _sb_band(qi, q2, k_ref, v_ref, u_ref, o_ref, *, w):
    t = 2 * w

    def block(ref, kb):
        return ref[0, pl.ds(pl.multiple_of(kb * w, w), w), :]

    b0 = 2 * qi
    bm = jnp.maximum(b0 - 1, 0)
    q_a = jnp.concatenate([q2[0:w], q2[t:t + w]], axis=0)
    q_b = jnp.concatenate([q2[w:t], q2[t + w:2 * t]], axis=0)
    z_0 = lax.dot_general(jnp.concatenate([q_a, q_b], axis=0), block(k_ref, b0), _NT,
                          preferred_element_type=F32)
    z_b_diag = lax.dot_general(q_b, block(k_ref, b0 + 1), _NT, preferred_element_type=F32)
    z_a_left = lax.dot_general(q_a, block(k_ref, bm), _NT, preferred_element_type=F32)
    causal = (lax.broadcasted_iota(jnp.int32, (t, w), 1)
              < (lax.broadcasted_iota(jnp.int32, (t, w), 0) & (w - 1)))
    z_first = jnp.concatenate([jnp.where(causal, z_0[:t], NEG_BIG),
                               jnp.where(causal, z_b_diag, NEG_BIG)], axis=0)
    z_second = jnp.concatenate([jnp.where(qi > 0, z_a_left, NEG_BIG), z_0[t:]], axis=0)

    def suffix_sums(z):
        sp = _softplus2(z)
        hi = sp.astype(BF16)
        lo = (sp - hi.astype(F32)).astype(BF16)
        return jnp.dot(jnp.concatenate([hi, lo], axis=1), u_ref[...], preferred_element_type=F32)

    cum_first = suffix_sums(z_first)
    cum_second = suffix_sums(z_second)
    tot_first = jnp.broadcast_to(cum_first[:, 0:1], (2 * t, LANES))
    carry = tot_first + jnp.broadcast_to(cum_second[:, 0:1], (2 * t, LANES))
    a_first = jnp.exp2(z_first - cum_first).astype(BF16)
    a_second = jnp.exp2(z_second - cum_second
                        - jnp.concatenate([tot_first] * (w // LANES), axis=1)).astype(BF16)
    v_0 = block(v_ref, b0)
    acc_a = (jnp.dot(a_first[:t], v_0, preferred_element_type=F32)
             + jnp.dot(a_second[:t], block(v_ref, bm), preferred_element_type=F32))
    acc_b = (jnp.dot(a_first[t:], block(v_ref, b0 + 1), preferred_element_type=F32)
             + jnp.dot(a_second[t:], v_0, preferred_element_type=F32))
    lane = lax.broadcasted_iota(jnp.int32, (w, LANES), 1)
    o_ref[...] = jnp.concatenate([jnp.where(lane < HEAD_DIM, acc_a[:w], acc_a[w:]),
                                  jnp.where(lane < HEAD_DIM, acc_b[:w], acc_b[w:])], axis=0)
    return jnp.logical_or(qi == 0, jnp.min(carry) >= ZERO_WEIGHT_CARRY)


def _sb_walk(qi, q2, row, col, top, k_ref, v_ref, u_ref, o_ref, acc_ref, carry_ref, z_ref, hilo_ref,
             arg_ref, *, t, w):
    rows = 2 * t
    acc_ref[...] = jnp.zeros_like(acc_ref)
    carry_ref[...] = jnp.zeros_like(carry_ref)

    def scores(kb, masked, dst):
        kt = k_ref[0, pl.ds(pl.multiple_of(kb * w, w), w), :]
        z = lax.dot_general(q2, kt, _NT, preferred_element_type=F32)
        if masked:
            z = jnp.where(col + (kb * w - qi * t) < row, z, NEG_BIG)
        sp = _softplus2(z)
        hi = sp.astype(BF16)
        lo = (sp - hi.astype(F32)).astype(BF16)
        hilo_ref[dst] = jnp.concatenate([hi, lo], axis=1)
        z_ref[dst] = z

    def sums(src, dst):
        cum = jnp.dot(hilo_ref[src], u_ref[...], preferred_element_type=F32)
        c = carry_ref[...]
        arg_ref[dst] = z_ref[src] - cum - jnp.concatenate([c] * (w // LANES), axis=1)
        c = c + jnp.broadcast_to(cum[:, 0:1], (rows, LANES))
        carry_ref[...] = c
        return jnp.min(c)

    def values(kb, src):
        vt = v_ref[0, pl.ds(pl.multiple_of(kb * w, w), w), :]
        acc_ref[...] += jnp.dot(jnp.exp2(arg_ref[src]).astype(BF16), vt, preferred_element_type=F32)

    scores(top, True, 0)
    min_carry = sums(0, 0)
    scores(top - 1, True, 1)

    def live(state):
        p, min_carry = state
        return jnp.logical_and(p < (top - 1) // 2, min_carry < ZERO_WEIGHT_CARRY)

    def pair(state):
        p, _ = state
        kb = top - 2 * p
        scores(kb - 2, False, 0)
        values(kb, 0)
        sums(1, 1)
        scores(kb - 3, False, 1)
        values(kb - 1, 1)
        return p + 1, sums(0, 0)

    p, min_carry = lax.while_loop(live, pair, (jnp.int32(0), min_carry))
    kb = top - 2 * p
    values(kb, 0)

    @pl.when(min_carry < ZERO_WEIGHT_CARRY)
    def _():
        sums(1, 1)
        values(kb - 1, 1)

    lane = lax.broadcasted_iota(jnp.int32, (t, LANES), 1)
    o_ref[...] = jnp.where(lane < HEAD_DIM, acc_ref[:t], acc_ref[t:])


def _sb_attn(qkv, u2, t, w, tiles_per_step):
    b, s, _ = qkv.shape
    rows = 2 * t
    tq = tiles_per_step * t
    return pl.pallas_call(
        functools.partial(_sb_kernel, t=t, w=w),
        grid=(b, SB_HEADS // 2, s // tq),
        in_specs=[
            pl.BlockSpec((1, tq, LANES), lambda bi, p, qi: (bi, qi, SB_Q_BLK + p)),
            pl.BlockSpec((1, s, LANES), lambda bi, p, qi: (bi, 0, SB_K_BLK + p)),
            pl.BlockSpec((1, s, LANES), lambda bi, p, qi: (bi, 0, SB_V_BLK + p)),
            pl.BlockSpec((2 * w, w), lambda bi, p, qi: (0, 0)),
        ],
        out_specs=pl.BlockSpec((1, tq, LANES), lambda bi, p, qi: (bi, qi, p)),
        out_shape=jax.ShapeDtypeStruct((b, s, SB_WIDTH), F32),
        scratch_shapes=[
            pltpu.VMEM((rows, LANES), F32),
            pltpu.VMEM((rows, LANES), F32),
            pltpu.VMEM((2, rows, w), F32),
            pltpu.VMEM((2, rows, 2 * w), BF16),
            pltpu.VMEM((2, rows, w), F32),
        ],
        compiler_params=pltpu.CompilerParams(
            dimension_semantics=("arbitrary", "arbitrary", "arbitrary"),
            vmem_limit_bytes=VMEM_LIMIT_BYTES),
        name="sb_attn",
    )(qkv, qkv, qkv, u2)


def _diff_kernel(q_ref, k_ref, v_ref, lam_ref, slope_ref, g_ref, o_ref, acc_ref, m_ref, alpha_ref,
                 mtile_ref, s_ref, knorm_ref, *, t, out_scale):
    step = pl.program_id(2)
    n_sub = q_ref.shape[1] // t
    slope2 = slope_ref[pl.program_id(1)]
    lane = lax.broadcasted_iota(jnp.int32, (t, LANES), 1)
    lo_half = lane < HEAD_DIM

    @pl.when(step == 0)
    def _():
        def tile_norms(j, best):
            kt = k_ref[0, pl.ds(pl.multiple_of(j * t, t), t), :].astype(F32)
            sq = kt * kt
            n1 = jnp.max(jnp.sum(jnp.where(lo_half, sq, 0.0), axis=-1, keepdims=True))
            n2 = jnp.max(jnp.sum(jnp.where(lo_half, 0.0, sq), axis=-1, keepdims=True))
            best = (jnp.maximum(best[0], n1), jnp.maximum(best[1], n2))
            knorm_ref[0, j] = best[0]
            knorm_ref[1, j] = best[1]
            return best

        lax.fori_loop(0, k_ref.shape[1] // t, tile_norms, (jnp.zeros((), F32), jnp.zeros((), F32)))

    causal = ((lax.broadcasted_iota(jnp.int32, (2 * t, t), 0) & (t - 1))
              >= lax.broadcasted_iota(jnp.int32, (2 * t, t), 1))
    tiles = [_diff_tile(step * n_sub + sub, q_ref[0, sub * t:(sub + 1) * t, :], causal, slope2, k_ref, v_ref,
                        lam_ref, g_ref, o_ref.at[0, sub * t:(sub + 1) * t, :], acc_ref.at[sub],
                        m_ref.at[sub], alpha_ref.at[sub], mtile_ref.at[sub], s_ref.at[sub], knorm_ref,
                        t=t, out_scale=out_scale) for sub in range(n_sub)]
    for tile in tiles:
        next(tile)
    for tile in tiles:
        next(tile, None)


def _diff_tile(qi, q_tile, causal, slope2, k_ref, v_ref, lam_ref, g_ref, o_ref, acc_ref, m_ref, alpha_ref,
               mtile_ref, s_ref, knorm_ref, *, t, out_scale):
    rows = 2 * t
    reps = t // LANES
    q2 = _stack_halves(q_tile)
    key_iota = lax.broadcasted_iota(jnp.int32, (1, t), 1)
    ones = jnp.ones((t, LANES), BF16)

    acc_ref[...] = jnp.zeros_like(acc_ref)
    m_ref[...] = jnp.full_like(m_ref, NEG_BIG)

    def qk(kb):
        kt = k_ref[0, pl.ds(pl.multiple_of(kb * t, t), t), :]
        return lax.dot_general(q2, kt, _NT, preferred_element_type=F32)

    def scores(kb, valid, dst):
        bias = (key_iota + (kb - qi) * t).astype(F32) * slope2
        s2 = qk(kb) + bias
        if valid is not None:
            s2 = jnp.where(valid, s2, NEG_BIG)
        m_old = m_ref[...]
        m_new = jnp.maximum(m_old, jnp.max(s2, axis=-1, keepdims=True))
        s_ref[dst] = s2
        mtile_ref[dst] = m_new
        alpha_ref[dst] = jnp.exp2(m_old - m_new)
        m_ref[...] = m_new

    def min_running_max():
        m = m_ref[...]
        return jnp.min(m[:t]), jnp.min(m[t:])

    def values(kb, src):
        vt = v_ref[0, pl.ds(pl.multiple_of(kb * t, t), t), :]
        alpha = alpha_ref[src]
        p = jnp.exp2(s_ref[src] - jnp.concatenate([mtile_ref[src]] * reps, axis=1))
        acc_ref[...] = (acc_ref[...] * jnp.concatenate([alpha, alpha], axis=1)
                        + jnp.dot(p.astype(BF16), jnp.concatenate([vt, ones], axis=1),
                                  preferred_element_type=F32))

    q_sq = q2.astype(F32)
    q_sq = jnp.sum(q_sq * q_sq, axis=-1, keepdims=True)
    qn1 = jnp.max(q_sq[:t])
    qn2 = jnp.max(q_sq[t:])

    def all_zero(kb, mmin1, mmin2):
        bias_max = ((kb - qi) * t + (t - 1)).astype(F32) * slope2

        def below(qn, kn, mmin):
            room = mmin - ZERO_WEIGHT_MARGIN - bias_max
            return jnp.logical_and(room > 0.0, qn * kn <= room * room)

        return jnp.logical_and(below(qn1, knorm_ref[0, kb], mmin1), below(qn2, knorm_ref[1, kb], mmin2))

    scores(qi, causal, 0)
    first_mmin = min_running_max()
    yield

    def needed(j, mmin):
        kb = jnp.maximum(qi - j, 0)
        return jnp.logical_and(j <= qi, jnp.logical_not(all_zero(kb, *mmin)))

    def live(state):
        return needed(state[0] + 2, state[1:])

    def pair(state):
        j = state[0]
        mmin = min_running_max()
        scores(qi - j - 1, None, 1)
        values(qi - j, 0)
        scores(qi - j - 2, None, 0)
        values(qi - j - 1, 1)
        return (j + 2,) + mmin

    state = lax.while_loop(live, pair, (jnp.int32(0),) + first_mmin)
    j = state[0]
    one_more = needed(j + 1, min_running_max())

    @pl.when(one_more)
    def _():
        scores(qi - j - 1, None, 1)
        values(qi - j, 0)
        values(qi - j - 1, 1)

    @pl.when(jnp.logical_not(one_more))
    def _():
        values(qi - j, 0)

    a1 = acc_ref[:t]
    a2 = acc_ref[t:]
    o = a1[:, :LANES] / a1[:, LANES:] - lam_ref[0, 0] * (a2[:, :LANES] / a2[:, LANES:])
    o_ref[...] = _rms_norm(o, g_ref[...]) * out_scale


def _diff_attn(qkv, lam, subln_g, t, out_scale, tiles_per_step):
    b, s, _ = qkv.shape
    rows = 2 * t
    n = tiles_per_step
    tq = n * t
    slopes2 = jnp.asarray([2.0 ** (-8.0 * (h + 1) / DIFF_HEADS) * LOG2E for h in range(DIFF_HEADS)], F32)
    return pl.pallas_call(
        functools.partial(_diff_kernel, t=t, out_scale=out_scale),
        grid=(b, DIFF_HEADS, s // tq),
        in_specs=[
            pl.BlockSpec((1, tq, LANES), lambda bi, h, qi: (bi, qi, DF_Q_BLK + h)),
            pl.BlockSpec((1, s, LANES), lambda bi, h, qi: (bi, 0, DF_K_BLK + h)),
            pl.BlockSpec((1, s, LANES), lambda bi, h, qi: (bi, 0, DF_V_BLK + h)),
            pl.BlockSpec(memory_space=pltpu.SMEM),
            pl.BlockSpec(memory_space=pltpu.SMEM),
            pl.BlockSpec((1, LANES), lambda bi, h, qi: (0, 0)),
        ],
        out_specs=pl.BlockSpec((1, tq, LANES), lambda bi, h, qi: (bi, qi, h)),
        out_shape=jax.ShapeDtypeStruct((b, s, DIFF_WIDTH), F32),
        scratch_shapes=[
            pltpu.VMEM((n, rows, 2 * LANES), F32),
            pltpu.VMEM((n, rows, LANES), F32),
            pltpu.VMEM((n, 2, rows, LANES), F32),
            pltpu.VMEM((n, 2, rows, LANES), F32),
            pltpu.VMEM((n, 2, rows, t), F32),
            pltpu.SMEM((2, s // t), F32),
        ],
        compiler_params=pltpu.CompilerParams(
            dimension_semantics=("arbitrary", "arbitrary", "arbitrary"),
            vmem_limit_bytes=VMEM_LIMIT_BYTES),
        name="diff_attn",
    )(qkv, qkv, qkv, lam, slopes2, subln_g)


def _lambda_kernel(q1_ref, k1_ref, q2_ref, k2_ref, o_ref, *, lambda_init):
    s1 = jnp.sum(q1_ref[...] * k1_ref[...], axis=-1, keepdims=True)
    s2 = jnp.sum(q2_ref[...] * k2_ref[...], axis=-1, keepdims=True)
    o_ref[...] = jnp.exp(s1) - jnp.exp(s2) + lambda_init


def _lambda(q1, k1, q2, k2, lambda_init):
    return pl.pallas_call(
        functools.partial(_lambda_kernel, lambda_init=lambda_init),
        out_shape=jax.ShapeDtypeStruct((1, 1), F32),
        name="diff_lambda",
    )(q1, k1, q2, k2)


def _out_kernel(x_ref, sb_ref, df_ref, g_pre_ref, w_zg_ref, b_ref, w_sb_ref, w_df_ref, w_out_ref,
                g_post_ref, o_ref):
    x = x_ref[...]
    h = _rms_norm(x, g_pre_ref[...]).astype(BF16)
    zg = jnp.dot(h, w_zg_ref[...], preferred_element_type=F32)
    d = x.shape[-1]
    sb_z = zg[:, :SB_WIDTH]
    df_z = zg[:, SB_WIDTH:SB_WIDTH + DIFF_WIDTH]
    gates = jax.nn.sigmoid(zg[:, SB_WIDTH + DIFF_WIDTH:] + b_ref[...])
    y_sb = jnp.dot((sb_ref[...] * (sb_z * jax.nn.sigmoid(sb_z))).astype(BF16), w_sb_ref[...],
                   preferred_element_type=F32)
    y_df = jnp.dot((df_ref[...] * (df_z * jax.nn.sigmoid(df_z))).astype(BF16), w_df_ref[...],
                   preferred_element_type=F32)
    merged = gates[:, :d] * y_sb + gates[:, d:] * y_df
    out = jnp.dot(merged.astype(BF16), w_out_ref[...], preferred_element_type=F32)
    o_ref[...] = x + _rms_norm(out, g_post_ref[...])


def _out_proj(x2d, sb, df, g_pre, w_zg, b_gate, w_sb, w_df, w_out, g_post, tm):
    n, d = x2d.shape
    zg_cols = w_zg.shape[1]
    full = lambda shape: pl.BlockSpec(shape, lambda i: (0, 0))
    return pl.pallas_call(
        _out_kernel,
        grid=(n // tm,),
        in_specs=[
            pl.BlockSpec((tm, d), lambda i: (i, 0)),
            pl.BlockSpec((tm, SB_WIDTH), lambda i: (i, 0)),
            pl.BlockSpec((tm, DIFF_WIDTH), lambda i: (i, 0)),
            full((1, d)),
            full((d, zg_cols)),
            full((1, 2 * d)),
            full((SB_WIDTH, d)),
            full((DIFF_WIDTH, d)),
            full((d, d)),
            full((1, d)),
        ],
        out_specs=pl.BlockSpec((tm, d), lambda i: (i, 0)),
        out_shape=jax.ShapeDtypeStruct((n, d), F32),
        compiler_params=pltpu.CompilerParams(
            dimension_semantics=("arbitrary",), vmem_limit_bytes=VMEM_LIMIT_BYTES),
        name="out_proj",
    )(x2d, sb, df, g_pre, w_zg, b_gate, w_sb, w_df, w_out, g_post)


def _split_w_in(w):
    sizes = (SB_WIDTH,) * 4 + (DIFF_WIDTH,) * 4 + (w.shape[0],) * 2
    parts, off = [], 0
    for n in sizes:
        parts.append(w[:, off:off + n])
        off += n
    return parts


def kernel(x, pre_norm_g, w_in, b_gate, lambda_q1, lambda_k1, lambda_q2, lambda_k2, subln_g, w_o_sb,
           w_o_diff, w_out, post_norm_g):
    b, s, d = x.shape
    depth = w_in.shape[0]
    t, w = ATTN_T, SUM_W
    assert s % t == 0 and s >= 2 * t and t % w == 0
    tm = min(512, b * s)
    u = (lax.broadcasted_iota(jnp.int32, (w, w), 0)
         >= lax.broadcasted_iota(jnp.int32, (w, w), 1)).astype(BF16)
    u2 = jnp.concatenate([u, u], axis=0)
    q_scale = jnp.full((1, SB_WIDTH), QK_SCALE * LOG2E, F32)
    one = jnp.ones((1, SB_WIDTH), F32)
    col_scale = jnp.concatenate([q_scale, one, one, q_scale, one, one], axis=1)
    for layer in range(depth):
        sb_q, sb_k, sb_v, sb_z, df_q, df_k, df_v, df_z, g_sb, g_df = _split_w_in(w_in[layer])
        w_qkv = jnp.concatenate([sb_q, sb_k, sb_v, df_q, df_k, df_v], axis=1).astype(BF16)
        w_zg = jnp.concatenate([sb_z, df_z, g_sb, g_df], axis=1).astype(BF16)
        lambda_init = 0.8 - 0.6 * math.exp(-0.3 * layer)
        x2d = x.reshape(b * s, d)
        g_pre = pre_norm_g[layer].reshape(1, d)

        qkv = _qkv_proj(x2d, g_pre, w_qkv, col_scale, tm).reshape(b, s, QKV_COLS)
        sb_out = _sb_attn(qkv, u2, t, w, SB_TILES_PER_STEP if s % (SB_TILES_PER_STEP * t) == 0 else 1)
        lam = _lambda(lambda_q1[layer][None], lambda_k1[layer][None], lambda_q2[layer][None],
                      lambda_k2[layer][None], lambda_init)
        df_out = _diff_attn(qkv, lam, subln_g[layer].reshape(1, LANES), t, 1.0 - lambda_init,
                            DIFF_TILES_PER_STEP if s % (DIFF_TILES_PER_STEP * t) == 0 else 1)
        x = _out_proj(x2d, sb_out.reshape(b * s, SB_WIDTH), df_out.reshape(b * s, DIFF_WIDTH), g_pre,
                      w_zg, b_gate[layer].reshape(1, 2 * d), w_o_sb[layer].astype(BF16),
                      w_o_diff[layer].astype(BF16), w_out[layer].astype(BF16),
                      post_norm_g[layer].reshape(1, d), tm).reshape(b, s, d)
    return x
```

```python
import functools
import math

import jax
import jax.numpy as jnp
from jax import lax
from jax.experimental import pallas as pl
from jax.experimental.pallas import tpu as pltpu

F32 = jnp.float32
BF16 = jnp.bfloat16

HEAD_DIM = 64
SB_HEADS = 8
DIFF_HEADS = 4
SB_WIDTH = SB_HEADS * HEAD_DIM
DIFF_WIDTH = DIFF_HEADS * 2 * HEAD_DIM
NORM_EPS = 1e-6
LANES = 128
MXU_DIM = 256
LOG2E = 1.4426950408889634
QK_SCALE = HEAD_DIM ** -0.5
NEG_BIG = -1e30
SOFTPLUS_CLAMP = 100.0
ZERO_WEIGHT_CARRY = 152.0
ZERO_WEIGHT_MARGIN = 156.0
VMEM_LIMIT_BYTES = 56 * 1024 * 1024

SUM_W = MXU_DIM
ATTN_T = 2 * SUM_W
SB_TILES_PER_STEP = 2
DIFF_TILES_PER_STEP = 2

QKV_COLS = 3 * SB_WIDTH + 3 * DIFF_WIDTH
SB_Q_BLK, SB_K_BLK, SB_V_BLK = 0, 4, 8
DF_Q_BLK, DF_K_BLK, DF_V_BLK = 12, 16, 20

_NT = (((1,), (1,)), ((), ()))


def _rms_norm(x32, g32):
    return x32 * lax.rsqrt(jnp.mean(x32 * x32, axis=-1, keepdims=True) + NORM_EPS) * g32


def _stack_halves(q):
    q32 = q.astype(F32)
    lane = lax.broadcasted_iota(jnp.int32, q.shape, 1)
    return jnp.concatenate([jnp.where(lane < HEAD_DIM, q32, 0.0).astype(BF16),
                            jnp.where(lane >= HEAD_DIM, q32, 0.0).astype(BF16)], axis=0)


def _qkv_kernel(x_ref, g_ref, w_ref, cs_ref, o_ref):
    h = _rms_norm(x_ref[...], g_ref[...]).astype(BF16)
    p = jnp.dot(h, w_ref[...], preferred_element_type=F32)
    o_ref[...] = (p * cs_ref[...]).astype(BF16)


def _qkv_proj(x2d, g, w_qkv, col_scale, tm):
    n, d = x2d.shape
    return pl.pallas_call(
        _qkv_kernel,
        grid=(n // tm,),
        in_specs=[
            pl.BlockSpec((tm, d), lambda i: (i, 0)),
            pl.BlockSpec((1, d), lambda i: (0, 0)),
            pl.BlockSpec((d, QKV_COLS), lambda i: (0, 0)),
            pl.BlockSpec((1, QKV_COLS), lambda i: (0, 0)),
        ],
        out_specs=pl.BlockSpec((tm, QKV_COLS), lambda i: (i, 0)),
        out_shape=jax.ShapeDtypeStruct((n, QKV_COLS), BF16),
        compiler_params=pltpu.CompilerParams(
            dimension_semantics=("arbitrary",), vmem_limit_bytes=VMEM_LIMIT_BYTES),
        name="qkv_proj",
    )(x2d, g, w_qkv, col_scale)


def _sb_kernel(q_ref, k_ref, v_ref, u_ref, o_ref, acc_ref, carry_ref, z_ref, hilo_ref, arg_ref, *,
               t, w):
    rows = 2 * t
    row = lax.broadcasted_iota(jnp.int32, (rows, w), 0) & (t - 1)
    col = lax.broadcasted_iota(jnp.int32, (rows, w), 1)
    n_sub = q_ref.shape[1] // t

    tiles = []
    for sub in range(n_sub):
        qi = pl.program_id(2) * n_sub + sub
        q2 = _stack_halves(q_ref[0, sub * t:(sub + 1) * t, :])
        o_tile = o_ref.at[0, sub * t:(sub + 1) * t, :]
        tiles.append((qi, q2, o_tile, _sb_band(qi, q2, k_ref, v_ref, u_ref, o_tile, w=w)))

    for qi, q2, o_tile, done in tiles:
        @pl.when(jnp.logical_not(done))
        def _(qi=qi, q2=q2, o_tile=o_tile):
            top = (qi + 1) * (t // w) - 1
            _sb_walk(qi, q2, row, col, top, k_ref, v_ref, u_ref, o_tile, acc_ref, carry_ref, z_ref,
                     hilo_ref, arg_ref, t=t, w=w)


def _softplus2(z):
    return jnp.maximum(z, jnp.log(1.0 + jnp.exp2(jnp.minimum(z, SOFTPLUS_CLAMP))) * LOG2E)


def _sb_band(qi, q2, k_ref, v_ref, u_ref, o_ref, *, w):
    t = 2 * w

    def block(ref, kb):
        return ref[0, pl.ds(pl.multiple_of(kb * w, w), w), :]

    b0 = 2 * qi
    bm = jnp.maximum(b0 - 1, 0)
    q_a = jnp.concatenate([q2[0:w], q2[t:t + w]], axis=0)
    q_b = jnp.concatenate([q2[w:t], q2[t + w:2 * t]], axis=0)
    z_0 = lax.dot_general(jnp.concatenate([q_a, q_b], axis=0), block(k_ref, b0), _NT,
                          preferred_element_type=F32)
    z_b_diag = lax.dot_general(q_b, block(k_ref, b0 + 1), _NT, preferred_element_type=F32)
    z_a_left = lax.dot_general(q_a, block(k_ref, bm), _NT, preferred_element_type=F32)
    causal = (lax.broadcasted_iota(jnp.int32, (t, w), 1)
              < (lax.broadcasted_iota(jnp.int32, (t, w), 0) & (w - 1)))
    z_first = jnp.concatenate([jnp.where(causal, z_0[:t], NEG_BIG),
                               jnp.where(causal, z_b_diag, NEG_BIG)], axis=0)
    z_second = jnp.concatenate([jnp.where(qi > 0, z_a_left, NEG_BIG), z_0[t:]], axis=0)

    def suffix_sums(z):
        sp = _softplus2(z)
        hi = sp.astype(BF16)
        lo = (sp - hi.astype(F32)).astype(BF16)
        return jnp.dot(jnp.concatenate([hi, lo], axis=1), u_ref[...], preferred_element_type=F32)

    cum_first = suffix_sums(z_first)
    cum_second = suffix_sums(z_second)
    tot_first = jnp.broadcast_to(cum_first[:, 0:1], (2 * t, LANES))
    carry = tot_first + jnp.broadcast_to(cum_second[:, 0:1], (2 * t, LANES))
    a_first = jnp.exp2(z_first - cum_first).astype(BF16)
    a_second = jnp.exp2(z_second - cum_second
                        - jnp.concatenate([tot_first] * (w // LANES), axis=1)).astype(BF16)
    v_0 = block(v_ref, b0)
    acc_a = (jnp.dot(a_first[:t], v_0, preferred_element_type=F32)
             + jnp.dot(a_second[:t], block(v_ref, bm), preferred_element_type=F32))
    acc_b = (jnp.dot(a_first[t:], block(v_ref, b0 + 1), preferred_element_type=F32)
             + jnp.dot(a_second[t:], v_0, preferred_element_type=F32))
    lane = lax.broadcasted_iota(jnp.int32, (w, LANES), 1)
    o_ref[...] = jnp.concatenate([jnp.where(lane < HEAD_DIM, acc_a[:w], acc_a[w:]),
                                  jnp.where(lane < HEAD_DIM, acc_b[:w], acc_b[w:])], axis=0)
    return jnp.logical_or(qi == 0, jnp.min(carry) >= ZERO_WEIGHT_CARRY)


def _sb_walk(qi, q2, row, col, top, k_ref, v_ref, u_ref, o_ref, acc_ref, carry_ref, z_ref, hilo_ref,
             arg_ref, *, t, w):
    rows = 2 * t
    acc_ref[...] = jnp.zeros_like(acc_ref)
    carry_ref[...] = jnp.zeros_like(carry_ref)

    def scores(kb, masked, dst):
        kt = k_ref[0, pl.ds(pl.multiple_of(kb * w, w), w), :]
        z = lax.dot_general(q2, kt, _NT, preferred_element_type=F32)
        if masked:
            z = jnp.where(col + (kb * w - qi * t) < row, z, NEG_BIG)
        sp = _softplus2(z)
        hi = sp.astype(BF16)
        lo = (sp - hi.astype(F32)).astype(BF16)
        hilo_ref[dst] = jnp.concatenate([hi, lo], axis=1)
        z_ref[dst] = z

    def sums(src, dst):
        cum = jnp.dot(hilo_ref[src], u_ref[...], preferred_element_type=F32)
        c = carry_ref[...]
        arg_ref[dst] = z_ref[src] - cum - jnp.concatenate([c] * (w // LANES), axis=1)
        c = c + jnp.broadcast_to(cum[:, 0:1], (rows, LANES))
        carry_ref[...] = c
        return jnp.min(c)

    def values(kb, src):
        vt = v_ref[0, pl.ds(pl.multiple_of(kb * w, w), w), :]
        acc_ref[...] += jnp.dot(jnp.exp2(arg_ref[src]).astype(BF16), vt, preferred_element_type=F32)

    scores(top, True, 0)
    min_carry = sums(0, 0)
    scores(top - 1, True, 1)

    def live(state):
        p, min_carry = state
        return jnp.logical_and(p < (top - 1) // 2, min_carry < ZERO_WEIGHT_CARRY)

    def pair(state):
        p, _ = state
        kb = top - 2 * p
        scores(kb - 2, False, 0)
        values(kb, 0)
        sums(1, 1)
        scores(kb - 3, False, 1)
        values(kb - 1, 1)
        return p + 1, sums(0, 0)

    p, min_carry = lax.while_loop(live, pair, (jnp.int32(0), min_carry))
    kb = top - 2 * p
    values(kb, 0)

    @pl.when(min_carry < ZERO_WEIGHT_CARRY)
    def _():
        sums(1, 1)
        values(kb - 1, 1)

    lane = lax.broadcasted_iota(jnp.int32, (t, LANES), 1)
    o_ref[...] = jnp.where(lane < HEAD_DIM, acc_ref[:t], acc_ref[t:])


def _sb_attn(qkv, u2, t, w, tiles_per_step):
    b, s, _ = qkv.shape
    rows = 2 * t
    tq = tiles_per_step * t
    return pl.pallas_call(
        functools.partial(_sb_kernel, t=t, w=w),
        grid=(b, SB_HEADS // 2, s // tq),
        in_specs=[
            pl.BlockSpec((1, tq, LANES), lambda bi, p, qi: (bi, qi, SB_Q_BLK + p)),
            pl.BlockSpec((1, s, LANES), lambda bi, p, qi: (bi, 0, SB_K_BLK + p)),
            pl.BlockSpec((1, s, LANES), lambda bi, p, qi: (bi, 0, SB_V_BLK + p)),
            pl.BlockSpec((2 * w, w), lambda bi, p, qi: (0, 0)),
        ],
        out_specs=pl.BlockSpec((1, tq, LANES), lambda bi, p, qi: (bi, qi, p)),
        out_shape=jax.ShapeDtypeStruct((b, s, SB_WIDTH), F32),
        scratch_shapes=[
            pltpu.VMEM((rows, LANES), F32),
            pltpu.VMEM((rows, LANES), F32),
            pltpu.VMEM((2, rows, w), F32),
            pltpu.VMEM((2, rows, 2 * w), BF16),
            pltpu.VMEM((2, rows, w), F32),
        ],
        compiler_params=pltpu.CompilerParams(
            dimension_semantics=("arbitrary", "arbitrary", "arbitrary"),
            vmem_limit_bytes=VMEM_LIMIT_BYTES),
        name="sb_attn",
    )(qkv, qkv, qkv, u2)


def _diff_kernel(q_ref, k_ref, v_ref, lam_ref, slope_ref, g_ref, o_ref, acc_ref, m_ref, alpha_ref,
                 mtile_ref, s_ref, knorm_ref, *, t, out_scale):
    step = pl.program_id(2)
    n_sub = q_ref.shape[1] // t
    slope2 = slope_ref[pl.program_id(1)]
    lane = lax.broadcasted_iota(jnp.int32, (t, LANES), 1)
    lo_half = lane < HEAD_DIM

    @pl.when(step == 0)
    def _():
        def tile_norms(j, best):
            kt = k_ref[0, pl.ds(pl.multiple_of(j * t, t), t), :].astype(F32)
            sq = kt * kt
            n1 = jnp.max(jnp.sum(jnp.where(lo_half, sq, 0.0), axis=-1, keepdims=True))
            n2 = jnp.max(jnp.sum(jnp.where(lo_half, 0.0, sq), axis=-1, keepdims=True))
            best = (jnp.maximum(best[0], n1), jnp.maximum(best[1], n2))
            knorm_ref[0, j] = best[0]
            knorm_ref[1, j] = best[1]
            return best

        lax.fori_loop(0, k_ref.shape[1] // t, tile_norms, (jnp.zeros((), F32), jnp.zeros((), F32)))

    causal = ((lax.broadcasted_iota(jnp.int32, (2 * t, t), 0) & (t - 1))
              >= lax.broadcasted_iota(jnp.int32, (2 * t, t), 1))
    tiles = [_diff_tile(step * n_sub + sub, q_ref[0, sub * t:(sub + 1) * t, :], causal, slope2, k_ref, v_ref,
                        lam_ref, g_ref, o_ref.at[0, sub * t:(sub + 1) * t, :], acc_ref.at[sub],
                        m_ref.at[sub], alpha_ref.at[sub], mtile_ref.at[sub], s_ref.at[sub], knorm_ref,
                        t=t, out_scale=out_scale) for sub in range(n_sub)]
    for tile in tiles:
        next(tile)
    for tile in tiles:
        next(tile, None)


def _diff_tile(qi, q_tile, causal, slope2, k_ref, v_ref, lam_ref, g_ref, o_ref, acc_ref, m_ref, alpha_ref,
               mtile_ref, s_ref, knorm_ref, *, t, out_scale):
    rows = 2 * t
    reps = t // LANES
    q2 = _stack_halves(q_tile)
    key_iota = lax.broadcasted_iota(jnp.int32, (1, t), 1)
    ones = jnp.ones((t, LANES), BF16)

    acc_ref[...] = jnp.zeros_like(acc_ref)
    m_ref[...] = jnp.full_like(m_ref, NEG_BIG)

    def qk(kb):
        kt = k_ref[0, pl.ds(pl.multiple_of(kb * t, t), t), :]
        return lax.dot_general(q2, kt, _NT, preferred_element_type=F32)

    def scores(kb, valid, dst):
        bias = (key_iota + (kb - qi) * t).astype(F32) * slope2
        s2 = qk(kb) + bias
        if valid is not None:
            s2 = jnp.where(valid, s2, NEG_BIG)
        m_old = m_ref[...]
        m_new = jnp.maximum(m_old, jnp.max(s2, axis=-1, keepdims=True))
        s_ref[dst] = s2
        mtile_ref[dst] = m_new
        alpha_ref[dst] = jnp.exp2(m_old - m_new)
        m_ref[...] = m_new

    def min_running_max():
        m = m_ref[...]
        return jnp.min(m[:t]), jnp.min(m[t:])

    def values(kb, src):
        vt = v_ref[0, pl.ds(pl.multiple_of(kb * t, t), t), :]
        alpha = alpha_ref[src]
        p = jnp.exp2(s_ref[src] - jnp.concatenate([mtile_ref[src]] * reps, axis=1))
        acc_ref[...] = (acc_ref[...] * jnp.concatenate([alpha, alpha], axis=1)
                        + jnp.dot(p.astype(BF16), jnp.concatenate([vt, ones], axis=1),
                                  preferred_element_type=F32))

    q_sq = q2.astype(F32)
    q_sq = jnp.sum(q_sq * q_sq, axis=-1, keepdims=True)
    qn1 = jnp.max(q_sq[:t])
    qn2 = jnp.max(q_sq[t:])

    def all_zero(kb, mmin1, mmin2):
        bias_max = ((kb - qi) * t + (t - 1)).astype(F32) * slope2

        def below(qn, kn, mmin):
            room = mmin - ZERO_WEIGHT_MARGIN - bias_max
            return jnp.logical_and(room > 0.0, qn * kn <= room * room)

        return jnp.logical_and(below(qn1, knorm_ref[0, kb], mmin1), below(qn2, knorm_ref[1, kb], mmin2))

    scores(qi, causal, 0)
    first_mmin = min_running_max()
    yield

    def needed(j, mmin):
        kb = jnp.maximum(qi - j, 0)
        return jnp.logical_and(j <= qi, jnp.logical_not(all_zero(kb, *mmin)))

    def walk(n_pairs):
        def body(state):
            j = state[0]
            mmin = min_running_max()
            for i in range(n_pairs):
                scores(qi - j - 2 * i - 1, None, 1)
                values(qi - j - 2 * i, 0)
                scores(qi - j - 2 * i - 2, None, 0)
                values(qi - j - 2 * i - 1, 1)
            return (j + 2 * n_pairs,) + mmin

        return body

    state = (jnp.int32(0),) + first_mmin
    for n_pairs in (2, 1):
        state = lax.while_loop(lambda st, n=2 * n_pairs: needed(st[0] + n, st[1:]), walk(n_pairs), state)
    j = state[0]
    one_more = needed(j + 1, min_running_max())

    @pl.when(one_more)
    def _():
        scores(qi - j - 1, None, 1)
        values(qi - j, 0)
        values(qi - j - 1, 1)

    @pl.when(jnp.logical_not(one_more))
    def _():
        values(qi - j, 0)

    a1 = acc_ref[:t]
    a2 = acc_ref[t:]
    o = a1[:, :LANES] / a1[:, LANES:] - lam_ref[0, 0] * (a2[:, :LANES] / a2[:, LANES:])
    o_ref[...] = _rms_norm(o, g_ref[...]) * out_scale


def _diff_attn(qkv, lam, subln_g, t, out_scale, tiles_per_step):
    b, s, _ = qkv.shape
    rows = 2 * t
    n = tiles_per_step
    tq = n * t
    slopes2 = jnp.asarray([2.0 ** (-8.0 * (h + 1) / DIFF_HEADS) * LOG2E for h in range(DIFF_HEADS)], F32)
    return pl.pallas_call(
        functools.partial(_diff_kernel, t=t, out_scale=out_scale),
        grid=(b, DIFF_HEADS, s // tq),
        in_specs=[
            pl.BlockSpec((1, tq, LANES), lambda bi, h, qi: (bi, qi, DF_Q_BLK + h)),
            pl.BlockSpec((1, s, LANES), lambda bi, h, qi: (bi, 0, DF_K_BLK + h)),
            pl.BlockSpec((1, s, LANES), lambda bi, h, qi: (bi, 0, DF_V_BLK + h)),
            pl.BlockSpec(memory_space=pltpu.SMEM),
            pl.BlockSpec(memory_space=pltpu.SMEM),
            pl.BlockSpec((1, LANES), lambda bi, h, qi: (0, 0)),
        ],
        out_specs=pl.BlockSpec((1, tq, LANES), lambda bi, h, qi: (bi, qi, h)),
        out_shape=jax.ShapeDtypeStruct((b, s, DIFF_WIDTH), F32),
        scratch_shapes=[
            pltpu.VMEM((n, rows, 2 * LANES), F32),
            pltpu.VMEM((n, rows, LANES), F32),
            pltpu.VMEM((n, 2, rows, LANES), F32),
            pltpu.VMEM((n, 2, rows, LANES), F32),
            pltpu.VMEM((n, 2, rows, t), F32),
            pltpu.SMEM((2, s // t), F32),
        ],
        compiler_params=pltpu.CompilerParams(
            dimension_semantics=("arbitrary", "arbitrary", "arbitrary"),
            vmem_limit_bytes=VMEM_LIMIT_BYTES),
        name="diff_attn",
    )(qkv, qkv, qkv, lam, slopes2, subln_g)


def _lambda_kernel(q1_ref, k1_ref, q2_ref, k2_ref, o_ref, *, lambda_init):
    s1 = jnp.sum(q1_ref[...] * k1_ref[...], axis=-1, keepdims=True)
    s2 = jnp.sum(q2_ref[...] * k2_ref[...], axis=-1, keepdims=True)
    o_ref[...] = jnp.exp(s1) - jnp.exp(s2) + lambda_init


def _lambda(q1, k1, q2, k2, lambda_init):
    return pl.pallas_call(
        functools.partial(_lambda_kernel, lambda_init=lambda_init),
        out_shape=jax.ShapeDtypeStruct((1, 1), F32),
        name="diff_lambda",
    )(q1, k1, q2, k2)


def _out_kernel(x_ref, sb_ref, df_ref, g_pre_ref, w_zg_ref, b_ref, w_sb_ref, w_df_ref, w_out_ref,
                g_post_ref, o_ref):
    x = x_ref[...]
    h = _rms_norm(x, g_pre_ref[...]).astype(BF16)
    zg = jnp.dot(h, w_zg_ref[...], preferred_element_type=F32)
    d = x.shape[-1]
    sb_z = zg[:, :SB_WIDTH]
    df_z = zg[:, SB_WIDTH:SB_WIDTH + DIFF_WIDTH]
    gates = jax.nn.sigmoid(zg[:, SB_WIDTH + DIFF_WIDTH:] + b_ref[...])
    y_sb = jnp.dot((sb_ref[...] * (sb_z * jax.nn.sigmoid(sb_z))).astype(BF16), w_sb_ref[...],
                   preferred_element_type=F32)
    y_df = jnp.dot((df_ref[...] * (df_z * jax.nn.sigmoid(df_z))).astype(BF16), w_df_ref[...],
                   preferred_element_type=F32)
    merged = gates[:, :d] * y_sb + gates[:, d:] * y_df
    out = jnp.dot(merged.astype(BF16), w_out_ref[...], preferred_element_type=F32)
    o_ref[...] = x + _rms_norm(out, g_post_ref[...])


def _out_proj(x2d, sb, df, g_pre, w_zg, b_gate, w_sb, w_df, w_out, g_post, tm):
    n, d = x2d.shape
    zg_cols = w_zg.shape[1]
    full = lambda shape: pl.BlockSpec(shape, lambda i: (0, 0))
    return pl.pallas_call(
        _out_kernel,
        grid=(n // tm,),
        in_specs=[
            pl.BlockSpec((tm, d), lambda i: (i, 0)),
            pl.BlockSpec((tm, SB_WIDTH), lambda i: (i, 0)),
            pl.BlockSpec((tm, DIFF_WIDTH), lambda i: (i, 0)),
            full((1, d)),
            full((d, zg_cols)),
            full((1, 2 * d)),
            full((SB_WIDTH, d)),
            full((DIFF_WIDTH, d)),
            full((d, d)),
            full((1, d)),
        ],
        out_specs=pl.BlockSpec((tm, d), lambda i: (i, 0)),
        out_shape=jax.ShapeDtypeStruct((n, d), F32),
        compiler_params=pltpu.CompilerParams(
            dimension_semantics=("arbitrary",), vmem_limit_bytes=VMEM_LIMIT_BYTES),
        name="out_proj",
    )(x2d, sb, df, g_pre, w_zg, b_gate, w_sb, w_df, w_out, g_post)


def _split_w_in(w):
    sizes = (SB_WIDTH,) * 4 + (DIFF_WIDTH,) * 4 + (w.shape[0],) * 2
    parts, off = [], 0
    for n in sizes:
        parts.append(w[:, off:off + n])
        off += n
    return parts


def kernel(x, pre_norm_g, w_in, b_gate, lambda_q1, lambda_k1, lambda_q2, lambda_k2, subln_g, w_o_sb,
           w_o_diff, w_out, post_norm_g):
    b, s, d = x.shape
    depth = w_in.shape[0]
    t, w = ATTN_T, SUM_W
    assert s % t == 0 and s >= 2 * t and t % w == 0
    tm = min(512, b * s)
    u = (lax.broadcasted_iota(jnp.int32, (w, w), 0)
         >= lax.broadcasted_iota(jnp.int32, (w, w), 1)).astype(BF16)
    u2 = jnp.concatenate([u, u], axis=0)
    q_scale = jnp.full((1, SB_WIDTH), QK_SCALE * LOG2E, F32)
    one = jnp.ones((1, SB_WIDTH), F32)
    col_scale = jnp.concatenate([q_scale, one, one, q_scale, one, one], axis=1)
    for layer in range(depth):
        sb_q, sb_k, sb_v, sb_z, df_q, df_k, df_v, df_z, g_sb, g_df = _split_w_in(w_in[layer])
        w_qkv = jnp.concatenate([sb_q, sb_k, sb_v, df_q, df_k, df_v], axis=1).astype(BF16)
        w_zg = jnp.concatenate([sb_z, df_z, g_sb, g_df], axis=1).astype(BF16)
        lambda_init = 0.8 - 0.6 * math.exp(-0.3 * layer)
        x2d = x.reshape(b * s, d)
        g_pre = pre_norm_g[layer].reshape(1, d)

        qkv = _qkv_proj(x2d, g_pre, w_qkv, col_scale, tm).reshape(b, s, QKV_COLS)
        sb_out = _sb_attn(qkv, u2, t, w, SB_TILES_PER_STEP if s % (SB_TILES_PER_STEP * t) == 0 else 1)
        lam = _lambda(lambda_q1[layer][None], lambda_k1[layer][None], lambda_q2[layer][None],
                      lambda_k2[layer][None], lambda_init)
        df_out = _diff_attn(qkv, lam, subln_g[layer].reshape(1, LANES), t, 1.0 - lambda_init,
                            DIFF_TILES_PER_STEP if s % (DIFF_TILES_PER_STEP * t) == 0 else 1)
        x = _out_proj(x2d, sb_out.reshape(b * s, SB_WIDTH), df_out.reshape(b * s, DIFF_WIDTH), g_pre,
                      w_zg, b_gate[layer].reshape(1, 2 * d), w_o_sb[layer].astype(BF16),
                      w_o_diff[layer].astype(BF16), w_out[layer].astype(BF16),
                      post_norm_g[layer].reshape(1, d), tm).reshape(b, s, d)
    return x
```

```python
import functools
import math

import jax
import jax.numpy as jnp
from jax import lax
from jax.experimental import pallas as pl
from jax.experimental.pallas import tpu as pltpu

F32 = jnp.float32
BF16 = jnp.bfloat16

HEAD_DIM = 64
SB_HEADS = 8
DIFF_HEADS = 4
SB_WIDTH = SB_HEADS * HEAD_DIM
DIFF_WIDTH = DIFF_HEADS * 2 * HEAD_DIM
NORM_EPS = 1e-6
LANES = 128
MXU_DIM = 256
LOG2E = 1.4426950408889634
QK_SCALE = HEAD_DIM ** -0.5
NEG_BIG = -1e30
SOFTPLUS_CLAMP = 100.0
ZERO_WEIGHT_CARRY = 152.0
ZERO_WEIGHT_MARGIN = 156.0
VMEM_LIMIT_BYTES = 56 * 1024 * 1024

SUM_W = MXU_DIM
ATTN_T = 2 * SUM_W
SB_TILES_PER_STEP = 2
DIFF_TILES_PER_STEP = 2

QKV_COLS = 3 * SB_WIDTH + 3 * DIFF_WIDTH
SB_Q_BLK, SB_K_BLK, SB_V_BLK = 0, 4, 8
DF_Q_BLK, DF_K_BLK, DF_V_BLK = 12, 16, 20

_NT = (((1,), (1,)), ((), ()))


def _rms_norm(x32, g32):
    return x32 * lax.rsqrt(jnp.mean(x32 * x32, axis=-1, keepdims=True) + NORM_EPS) * g32


def _stack_halves(q):
    q32 = q.astype(F32)
    lane = lax.broadcasted_iota(jnp.int32, q.shape, 1)
    return jnp.concatenate([jnp.where(lane < HEAD_DIM, q32, 0.0).astype(BF16),
                            jnp.where(lane >= HEAD_DIM, q32, 0.0).astype(BF16)], axis=0)


def _qkv_kernel(x_ref, g_ref, w_ref, cs_ref, o_ref):
    h = _rms_norm(x_ref[...], g_ref[...]).astype(BF16)
    p = jnp.dot(h, w_ref[...], preferred_element_type=F32)
    o_ref[...] = (p * cs_ref[...]).astype(BF16)


def _qkv_proj(x2d, g, w_qkv, col_scale, tm):
    n, d = x2d.shape
    return pl.pallas_call(
        _qkv_kernel,
        grid=(n // tm,),
        in_specs=[
            pl.BlockSpec((tm, d), lambda i: (i, 0)),
            pl.BlockSpec((1, d), lambda i: (0, 0)),
            pl.BlockSpec((d, QKV_COLS), lambda i: (0, 0)),
            pl.BlockSpec((1, QKV_COLS), lambda i: (0, 0)),
        ],
        out_specs=pl.BlockSpec((tm, QKV_COLS), lambda i: (i, 0)),
        out_shape=jax.ShapeDtypeStruct((n, QKV_COLS), BF16),
        compiler_params=pltpu.CompilerParams(
            dimension_semantics=("arbitrary",), vmem_limit_bytes=VMEM_LIMIT_BYTES),
        name="qkv_proj",
    )(x2d, g, w_qkv, col_scale)


def _sb_kernel(q_ref, k_ref, v_ref, u_ref, o_ref, acc_ref, carry_ref, z_ref, hilo_ref, arg_ref, *,
               t, w):
    rows = 2 * t
    row = lax.broadcasted_iota(jnp.int32, (rows, w), 0) & (t - 1)
    col = lax.broadcasted_iota(jnp.int32, (rows, w), 1)
    n_sub = q_ref.shape[1] // t

    tiles = []
    for sub in range(n_sub):
        qi = pl.program_id(2) * n_sub + sub
        q2 = _stack_halves(q_ref[0, sub * t:(sub + 1) * t, :])
        o_tile = o_ref.at[0, sub * t:(sub + 1) * t, :]
        tiles.append((qi, q2, o_tile, _sb_band(qi, q2, k_ref, v_ref, u_ref, o_tile, w=w)))

    for qi, q2, o_tile, done in tiles:
        @pl.when(jnp.logical_not(done))
        def _(qi=qi, q2=q2, o_tile=o_tile):
            top = (qi + 1) * (t // w) - 1
            _sb_walk(qi, q2, row, col, top, k_ref, v_ref, u_ref, o_tile, acc_ref, carry_ref, z_ref,
                     hilo_ref, arg_ref, t=t, w=w)


def _softplus2(z):
    return jnp.maximum(z, jnp.log(1.0 + jnp.exp2(jnp.minimum(z, SOFTPLUS_CLAMP))) * LOG2E)


def _sb_band(qi, q2, k_ref, v_ref, u_ref, o_ref, *, w):
    t = 2 * w

    def block(ref, kb):
        return ref[0, pl.ds(pl.multiple_of(kb * w, w), w), :]

    b0 = 2 * qi
    bm = jnp.maximum(b0 - 1, 0)
    q_a = jnp.concatenate([q2[0:w], q2[t:t + w]], axis=0)
    q_b = jnp.concatenate([q2[w:t], q2[t + w:2 * t]], axis=0)
    z_0 = lax.dot_general(jnp.concatenate([q_a, q_b], axis=0), block(k_ref, b0), _NT,
                          preferred_element_type=F32)
    z_b_diag = lax.dot_general(q_b, block(k_ref, b0 + 1), _NT, preferred_element_type=F32)
    z_a_left = lax.dot_general(q_a, block(k_ref, bm), _NT, preferred_element_type=F32)
    causal = (lax.broadcasted_iota(jnp.int32, (t, w), 1)
              < (lax.broadcasted_iota(jnp.int32, (t, w), 0) & (w - 1)))
    z_first = jnp.concatenate([jnp.where(causal, z_0[:t], NEG_BIG),
                               jnp.where(causal, z_b_diag, NEG_BIG)], axis=0)
    z_second = jnp.concatenate([jnp.where(qi > 0, z_a_left, NEG_BIG), z_0[t:]], axis=0)

    def log_weights(z):
        sp = _softplus2(z)
        sp16 = sp.astype(BF16)
        right = jnp.dot(sp16, u_ref[2 * w:], preferred_element_type=F32)
        total = right[:, 0:1] + sp16[:, 0:1].astype(F32)
        return (z - sp) - right, jnp.broadcast_to(total, (2 * t, LANES))

    arg_first, tot_first = log_weights(z_first)
    arg_second, tot_second = log_weights(z_second)
    carry = tot_first + tot_second
    a_first = jnp.exp2(arg_first).astype(BF16)
    a_second = jnp.exp2(arg_second - jnp.concatenate([tot_first] * (w // LANES), axis=1)).astype(BF16)
    v_0 = block(v_ref, b0)
    acc_a = (jnp.dot(a_first[:t], v_0, preferred_element_type=F32)
             + jnp.dot(a_second[:t], block(v_ref, bm), preferred_element_type=F32))
    acc_b = (jnp.dot(a_first[t:], block(v_ref, b0 + 1), preferred_element_type=F32)
             + jnp.dot(a_second[t:], v_0, preferred_element_type=F32))
    lane = lax.broadcasted_iota(jnp.int32, (w, LANES), 1)
    o_ref[...] = jnp.concatenate([jnp.where(lane < HEAD_DIM, acc_a[:w], acc_a[w:]),
                                  jnp.where(lane < HEAD_DIM, acc_b[:w], acc_b[w:])], axis=0)
    return jnp.logical_or(qi == 0, jnp.min(carry) >= ZERO_WEIGHT_CARRY)


def _sb_walk(qi, q2, row, col, top, k_ref, v_ref, u_ref, o_ref, acc_ref, carry_ref, z_ref, hilo_ref,
             arg_ref, *, t, w):
    rows = 2 * t
    acc_ref[...] = jnp.zeros_like(acc_ref)
    carry_ref[...] = jnp.zeros_like(carry_ref)

    def scores(kb, masked, dst):
        kt = k_ref[0, pl.ds(pl.multiple_of(kb * w, w), w), :]
        z = lax.dot_general(q2, kt, _NT, preferred_element_type=F32)
        if masked:
            z = jnp.where(col + (kb * w - qi * t) < row, z, NEG_BIG)
        sp = _softplus2(z)
        hi = sp.astype(BF16)
        lo = (sp - hi.astype(F32)).astype(BF16)
        hilo_ref[dst] = jnp.concatenate([hi, lo], axis=1)
        z_ref[dst] = z

    def sums(src, dst):
        cum = jnp.dot(hilo_ref[src], u_ref[:2 * w], preferred_element_type=F32)
        c = carry_ref[...]
        arg_ref[dst] = z_ref[src] - cum - jnp.concatenate([c] * (w // LANES), axis=1)
        c = c + jnp.broadcast_to(cum[:, 0:1], (rows, LANES))
        carry_ref[...] = c
        return jnp.min(c)

    def values(kb, src):
        vt = v_ref[0, pl.ds(pl.multiple_of(kb * w, w), w), :]
        acc_ref[...] += jnp.dot(jnp.exp2(arg_ref[src]).astype(BF16), vt, preferred_element_type=F32)

    scores(top, True, 0)
    min_carry = sums(0, 0)
    scores(top - 1, True, 1)

    def live(state):
        p, min_carry = state
        return jnp.logical_and(p < (top - 1) // 2, min_carry < ZERO_WEIGHT_CARRY)

    def pair(state):
        p, _ = state
        kb = top - 2 * p
        scores(kb - 2, False, 0)
        values(kb, 0)
        sums(1, 1)
        scores(kb - 3, False, 1)
        values(kb - 1, 1)
        return p + 1, sums(0, 0)

    p, min_carry = lax.while_loop(live, pair, (jnp.int32(0), min_carry))
    kb = top - 2 * p
    values(kb, 0)

    @pl.when(min_carry < ZERO_WEIGHT_CARRY)
    def _():
        sums(1, 1)
        values(kb - 1, 1)

    lane = lax.broadcasted_iota(jnp.int32, (t, LANES), 1)
    o_ref[...] = jnp.where(lane < HEAD_DIM, acc_ref[:t], acc_ref[t:])


def _sb_attn(qkv, u2, t, w, tiles_per_step):
    b, s, _ = qkv.shape
    rows = 2 * t
    tq = tiles_per_step * t
    return pl.pallas_call(
        functools.partial(_sb_kernel, t=t, w=w),
        grid=(b, SB_HEADS // 2, s // tq),
        in_specs=[
            pl.BlockSpec((1, tq, LANES), lambda bi, p, qi: (bi, qi, SB_Q_BLK + p)),
            pl.BlockSpec((1, s, LANES), lambda bi, p, qi: (bi, 0, SB_K_BLK + p)),
            pl.BlockSpec((1, s, LANES), lambda bi, p, qi: (bi, 0, SB_V_BLK + p)),
            pl.BlockSpec((3 * w, w), lambda bi, p, qi: (0, 0)),
        ],
        out_specs=pl.BlockSpec((1, tq, LANES), lambda bi, p, qi: (bi, qi, p)),
        out_shape=jax.ShapeDtypeStruct((b, s, SB_WIDTH), F32),
        scratch_shapes=[
            pltpu.VMEM((rows, LANES), F32),
            pltpu.VMEM((rows, LANES), F32),
            pltpu.VMEM((2, rows, w), F32),
            pltpu.VMEM((2, rows, 2 * w), BF16),
            pltpu.VMEM((2, rows, w), F32),
        ],
        compiler_params=pltpu.CompilerParams(
            dimension_semantics=("arbitrary", "arbitrary", "arbitrary"),
            vmem_limit_bytes=VMEM_LIMIT_BYTES),
        name="sb_attn",
    )(qkv, qkv, qkv, u2)


def _diff_kernel(q_ref, k_ref, v_ref, lam_ref, slope_ref, g_ref, o_ref, acc_ref, m_ref, alpha_ref,
                 mtile_ref, s_ref, knorm_ref, *, t, out_scale):
    step = pl.program_id(2)
    n_sub = q_ref.shape[1] // t
    slope2 = slope_ref[pl.program_id(1)]
    lane = lax.broadcasted_iota(jnp.int32, (t, LANES), 1)
    lo_half = lane < HEAD_DIM

    @pl.when(step == 0)
    def _():
        def tile_norms(j, best):
            kt = k_ref[0, pl.ds(pl.multiple_of(j * t, t), t), :].astype(F32)
            sq = kt * kt
            n1 = jnp.max(jnp.sum(jnp.where(lo_half, sq, 0.0), axis=-1, keepdims=True))
            n2 = jnp.max(jnp.sum(jnp.where(lo_half, 0.0, sq), axis=-1, keepdims=True))
            best = (jnp.maximum(best[0], n1), jnp.maximum(best[1], n2))
            knorm_ref[0, j] = best[0]
            knorm_ref[1, j] = best[1]
            return best

        lax.fori_loop(0, k_ref.shape[1] // t, tile_norms, (jnp.zeros((), F32), jnp.zeros((), F32)))

    causal = ((lax.broadcasted_iota(jnp.int32, (2 * t, t), 0) & (t - 1))
              >= lax.broadcasted_iota(jnp.int32, (2 * t, t), 1))
    tiles = [_diff_tile(step * n_sub + sub, q_ref[0, sub * t:(sub + 1) * t, :], causal, slope2, k_ref, v_ref,
                        lam_ref, g_ref, o_ref.at[0, sub * t:(sub + 1) * t, :], acc_ref.at[sub],
                        m_ref.at[sub], alpha_ref.at[sub], mtile_ref.at[sub], s_ref.at[sub], knorm_ref,
                        t=t, out_scale=out_scale) for sub in range(n_sub)]
    for tile in tiles:
        next(tile)
    for tile in tiles:
        next(tile, None)


def _diff_tile(qi, q_tile, causal, slope2, k_ref, v_ref, lam_ref, g_ref, o_ref, acc_ref, m_ref, alpha_ref,
               mtile_ref, s_ref, knorm_ref, *, t, out_scale):
    rows = 2 * t
    reps = t // LANES
    q2 = _stack_halves(q_tile)
    key_iota = lax.broadcasted_iota(jnp.int32, (1, t), 1)
    ones = jnp.ones((t, LANES), BF16)

    acc_ref[...] = jnp.zeros_like(acc_ref)
    m_ref[...] = jnp.full_like(m_ref, NEG_BIG)

    def qk(kb):
        kt = k_ref[0, pl.ds(pl.multiple_of(kb * t, t), t), :]
        return lax.dot_general(q2, kt, _NT, preferred_element_type=F32)

    def scores(kb, valid, dst):
        bias = (key_iota + (kb - qi) * t).astype(F32) * slope2
        s2 = qk(kb) + bias
        if valid is not None:
            s2 = jnp.where(valid, s2, NEG_BIG)
        m_old = m_ref[...]
        m_new = jnp.maximum(m_old, jnp.max(s2, axis=-1, keepdims=True))
        s_ref[dst] = s2
        mtile_ref[dst] = m_new
        alpha_ref[dst] = jnp.exp2(m_old - m_new)
        m_ref[...] = m_new

    def min_running_max():
        m = m_ref[...]
        return jnp.min(m[:t]), jnp.min(m[t:])

    def values(kb, src):
        vt = v_ref[0, pl.ds(pl.multiple_of(kb * t, t), t), :]
        alpha = alpha_ref[src]
        p = jnp.exp2(s_ref[src] - jnp.concatenate([mtile_ref[src]] * reps, axis=1))
        acc_ref[...] = (acc_ref[...] * jnp.concatenate([alpha, alpha], axis=1)
                        + jnp.dot(p.astype(BF16), jnp.concatenate([vt, ones], axis=1),
                                  preferred_element_type=F32))

    q_sq = q2.astype(F32)
    q_sq = jnp.sum(q_sq * q_sq, axis=-1, keepdims=True)
    qn1 = jnp.max(q_sq[:t])
    qn2 = jnp.max(q_sq[t:])

    def all_zero(kb, mmin1, mmin2):
        bias_max = ((kb - qi) * t + (t - 1)).astype(F32) * slope2

        def below(qn, kn, mmin):
            room = mmin - ZERO_WEIGHT_MARGIN - bias_max
            return jnp.logical_and(room > 0.0, qn * kn <= room * room)

        return jnp.logical_and(below(qn1, knorm_ref[0, kb], mmin1), below(qn2, knorm_ref[1, kb], mmin2))

    scores(qi, causal, 0)
    first_mmin = min_running_max()
    yield

    def needed(j, mmin):
        kb = jnp.maximum(qi - j, 0)
        return jnp.logical_and(j <= qi, jnp.logical_not(all_zero(kb, *mmin)))

    def walk(n_pairs):
        def body(state):
            j = state[0]
            mmin = min_running_max()
            for i in range(n_pairs):
                scores(qi - j - 2 * i - 1, None, 1)
                values(qi - j - 2 * i, 0)
                scores(qi - j - 2 * i - 2, None, 0)
                values(qi - j - 2 * i - 1, 1)
            return (j + 2 * n_pairs,) + mmin

        return body

    state = (jnp.int32(0),) + first_mmin
    for n_pairs in (2, 1):
        state = lax.while_loop(lambda st, n=2 * n_pairs: needed(st[0] + n, st[1:]), walk(n_pairs), state)
    j = state[0]
    one_more = needed(j + 1, min_running_max())

    @pl.when(one_more)
    def _():
        scores(qi - j - 1, None, 1)
        values(qi - j, 0)
        values(qi - j - 1, 1)

    @pl.when(jnp.logical_not(one_more))
    def _():
        values(qi - j, 0)

    a1 = acc_ref[:t]
    a2 = acc_ref[t:]
    o = a1[:, :LANES] / a1[:, LANES:] - lam_ref[0, 0] * (a2[:, :LANES] / a2[:, LANES:])
    o_ref[...] = _rms_norm(o, g_ref[...]) * out_scale


def _diff_attn(qkv, lam, subln_g, t, out_scale, tiles_per_step):
    b, s, _ = qkv.shape
    rows = 2 * t
    n = tiles_per_step
    tq = n * t
    slopes2 = jnp.asarray([2.0 ** (-8.0 * (h + 1) / DIFF_HEADS) * LOG2E for h in range(DIFF_HEADS)], F32)
    return pl.pallas_call(
        functools.partial(_diff_kernel, t=t, out_scale=out_scale),
        grid=(b, DIFF_HEADS, s // tq),
        in_specs=[
            pl.BlockSpec((1, tq, LANES), lambda bi, h, qi: (bi, qi, DF_Q_BLK + h)),
            pl.BlockSpec((1, s, LANES), lambda bi, h, qi: (bi, 0, DF_K_BLK + h)),
            pl.BlockSpec((1, s, LANES), lambda bi, h, qi: (bi, 0, DF_V_BLK + h)),
            pl.BlockSpec(memory_space=pltpu.SMEM),
            pl.BlockSpec(memory_space=pltpu.SMEM),
            pl.BlockSpec((1, LANES), lambda bi, h, qi: (0, 0)),
        ],
        out_specs=pl.BlockSpec((1, tq, LANES), lambda bi, h, qi: (bi, qi, h)),
        out_shape=jax.ShapeDtypeStruct((b, s, DIFF_WIDTH), F32),
        scratch_shapes=[
            pltpu.VMEM((n, rows, 2 * LANES), F32),
            pltpu.VMEM((n, rows, LANES), F32),
            pltpu.VMEM((n, 2, rows, LANES), F32),
            pltpu.VMEM((n, 2, rows, LANES), F32),
            pltpu.VMEM((n, 2, rows, t), F32),
            pltpu.SMEM((2, s // t), F32),
        ],
        compiler_params=pltpu.CompilerParams(
            dimension_semantics=("arbitrary", "arbitrary", "arbitrary"),
            vmem_limit_bytes=VMEM_LIMIT_BYTES),
        name="diff_attn",
    )(qkv, qkv, qkv, lam, slopes2, subln_g)


def _lambda_kernel(q1_ref, k1_ref, q2_ref, k2_ref, o_ref, *, lambda_init):
    s1 = jnp.sum(q1_ref[...] * k1_ref[...], axis=-1, keepdims=True)
    s2 = jnp.sum(q2_ref[...] * k2_ref[...], axis=-1, keepdims=True)
    o_ref[...] = jnp.exp(s1) - jnp.exp(s2) + lambda_init


def _lambda(q1, k1, q2, k2, lambda_init):
    return pl.pallas_call(
        functools.partial(_lambda_kernel, lambda_init=lambda_init),
        out_shape=jax.ShapeDtypeStruct((1, 1), F32),
        name="diff_lambda",
    )(q1, k1, q2, k2)


def _out_kernel(x_ref, sb_ref, df_ref, g_pre_ref, w_zg_ref, b_ref, w_sb_ref, w_df_ref, w_out_ref,
                g_post_ref, o_ref):
    x = x_ref[...]
    h = _rms_norm(x, g_pre_ref[...]).astype(BF16)
    zg = jnp.dot(h, w_zg_ref[...], preferred_element_type=F32)
    d = x.shape[-1]
    sb_z = zg[:, :SB_WIDTH]
    df_z = zg[:, SB_WIDTH:SB_WIDTH + DIFF_WIDTH]
    gates = jax.nn.sigmoid(zg[:, SB_WIDTH + DIFF_WIDTH:] + b_ref[...])
    y_sb = jnp.dot((sb_ref[...] * (sb_z * jax.nn.sigmoid(sb_z))).astype(BF16), w_sb_ref[...],
                   preferred_element_type=F32)
    y_df = jnp.dot((df_ref[...] * (df_z * jax.nn.sigmoid(df_z))).astype(BF16), w_df_ref[...],
                   preferred_element_type=F32)
    merged = gates[:, :d] * y_sb + gates[:, d:] * y_df
    out = jnp.dot(merged.astype(BF16), w_out_ref[...], preferred_element_type=F32)
    o_ref[...] = x + _rms_norm(out, g_post_ref[...])


def _out_proj(x2d, sb, df, g_pre, w_zg, b_gate, w_sb, w_df, w_out, g_post, tm):
    n, d = x2d.shape
    zg_cols = w_zg.shape[1]
    full = lambda shape: pl.BlockSpec(shape, lambda i: (0, 0))
    return pl.pallas_call(
        _out_kernel,
        grid=(n // tm,),
        in_specs=[
            pl.BlockSpec((tm, d), lambda i: (i, 0)),
            pl.BlockSpec((tm, SB_WIDTH), lambda i: (i, 0)),
            pl.BlockSpec((tm, DIFF_WIDTH), lambda i: (i, 0)),
            full((1, d)),
            full((d, zg_cols)),
            full((1, 2 * d)),
            full((SB_WIDTH, d)),
            full((DIFF_WIDTH, d)),
            full((d, d)),
            full((1, d)),
        ],
        out_specs=pl.BlockSpec((tm, d), lambda i: (i, 0)),
        out_shape=jax.ShapeDtypeStruct((n, d), F32),
        compiler_params=pltpu.CompilerParams(
            dimension_semantics=("arbitrary",), vmem_limit_bytes=VMEM_LIMIT_BYTES),
        name="out_proj",
    )(x2d, sb, df, g_pre, w_zg, b_gate, w_sb, w_df, w_out, g_post)


def _split_w_in(w):
    sizes = (SB_WIDTH,) * 4 + (DIFF_WIDTH,) * 4 + (w.shape[0],) * 2
    parts, off = [], 0
    for n in sizes:
        parts.append(w[:, off:off + n])
        off += n
    return parts


def kernel(x, pre_norm_g, w_in, b_gate, lambda_q1, lambda_k1, lambda_q2, lambda_k2, subln_g, w_o_sb,
           w_o_diff, w_out, post_norm_g):
    b, s, d = x.shape
    depth = w_in.shape[0]
    t, w = ATTN_T, SUM_W
    assert s % t == 0 and s >= 2 * t and t % w == 0
    tm = min(512, b * s)
    tri_r = lax.broadcasted_iota(jnp.int32, (w, w), 0)
    tri_c = lax.broadcasted_iota(jnp.int32, (w, w), 1)
    u = (tri_r >= tri_c).astype(BF16)
    u2 = jnp.concatenate([u, u, (tri_r > tri_c).astype(BF16)], axis=0)
    q_scale = jnp.full((1, SB_WIDTH), QK_SCALE * LOG2E, F32)
    one = jnp.ones((1, SB_WIDTH), F32)
    col_scale = jnp.concatenate([q_scale, one, one, q_scale, one, one], axis=1)
    for layer in range(depth):
        sb_q, sb_k, sb_v, sb_z, df_q, df_k, df_v, df_z, g_sb, g_df = _split_w_in(w_in[layer])
        w_qkv = jnp.concatenate([sb_q, sb_k, sb_v, df_q, df_k, df_v], axis=1).astype(BF16)
        w_zg = jnp.concatenate([sb_z, df_z, g_sb, g_df], axis=1).astype(BF16)
        lambda_init = 0.8 - 0.6 * math.exp(-0.3 * layer)
        x2d = x.reshape(b * s, d)
        g_pre = pre_norm_g[layer].reshape(1, d)

        qkv = _qkv_proj(x2d, g_pre, w_qkv, col_scale, tm).reshape(b, s, QKV_COLS)
        sb_out = _sb_attn(qkv, u2, t, w, SB_TILES_PER_STEP if s % (SB_TILES_PER_STEP * t) == 0 else 1)
        lam = _lambda(lambda_q1[layer][None], lambda_k1[layer][None], lambda_q2[layer][None],
                      lambda_k2[layer][None], lambda_init)
        df_out = _diff_attn(qkv, lam, subln_g[layer].reshape(1, LANES), t, 1.0 - lambda_init,
                            DIFF_TILES_PER_STEP if s % (DIFF_TILES_PER_STEP * t) == 0 else 1)
        x = _out_proj(x2d, sb_out.reshape(b * s, SB_WIDTH), df_out.reshape(b * s, DIFF_WIDTH), g_pre,
                      w_zg, b_gate[layer].reshape(1, 2 * d), w_o_sb[layer].astype(BF16),
                      w_o_diff[layer].astype(BF16), w_out[layer].astype(BF16),
                      post_norm_g[layer].reshape(1, d), tm).reshape(b, s, d)
    return x
```

```python
import functools
import math

import jax
import jax.numpy as jnp
from jax import lax
from jax.experimental import pallas as pl
from jax.experimental.pallas import tpu as pltpu

F32 = jnp.float32
BF16 = jnp.bfloat16

HEAD_DIM = 64
SB_HEADS = 8
DIFF_HEADS = 4
SB_WIDTH = SB_HEADS * HEAD_DIM
DIFF_WIDTH = DIFF_HEADS * 2 * HEAD_DIM
NORM_EPS = 1e-6
LANES = 128
MXU_DIM = 256
LOG2E = 1.4426950408889634
QK_SCALE = HEAD_DIM ** -0.5
NEG_BIG = -1e30
SOFTPLUS_CLAMP = 100.0
ZERO_WEIGHT_CARRY = 152.0
ZERO_WEIGHT_MARGIN = 156.0
VMEM_LIMIT_BYTES = 56 * 1024 * 1024

SUM_W = MXU_DIM
ATTN_T = 2 * SUM_W
SB_TILES_PER_STEP = 2
DIFF_TILES_PER_STEP = 2

QKV_COLS = 3 * SB_WIDTH + 3 * DIFF_WIDTH
SB_Q_BLK, SB_K_BLK, SB_V_BLK = 0, 4, 8
DF_Q_BLK, DF_K_BLK, DF_V_BLK = 12, 16, 20

_NT = (((1,), (1,)), ((), ()))


def _rms_norm(x32, g32):
    return x32 * lax.rsqrt(jnp.mean(x32 * x32, axis=-1, keepdims=True) + NORM_EPS) * g32


def _stack_halves(q):
    q32 = q.astype(F32)
    lane = lax.broadcasted_iota(jnp.int32, q.shape, 1)
    return jnp.concatenate([jnp.where(lane < HEAD_DIM, q32, 0.0).astype(BF16),
                            jnp.where(lane >= HEAD_DIM, q32, 0.0).astype(BF16)], axis=0)


def _qkv_kernel(x_ref, g_ref, w_ref, cs_ref, o_ref):
    h = _rms_norm(x_ref[...], g_ref[...]).astype(BF16)
    p = jnp.dot(h, w_ref[...], preferred_element_type=F32)
    o_ref[...] = (p * cs_ref[...]).astype(BF16)


def _qkv_proj(x2d, g, w_qkv, col_scale, tm):
    n, d = x2d.shape
    return pl.pallas_call(
        _qkv_kernel,
        grid=(n // tm,),
        in_specs=[
            pl.BlockSpec((tm, d), lambda i: (i, 0)),
            pl.BlockSpec((1, d), lambda i: (0, 0)),
            pl.BlockSpec((d, QKV_COLS), lambda i: (0, 0)),
            pl.BlockSpec((1, QKV_COLS), lambda i: (0, 0)),
        ],
        out_specs=pl.BlockSpec((tm, QKV_COLS), lambda i: (i, 0)),
        out_shape=jax.ShapeDtypeStruct((n, QKV_COLS), BF16),
        compiler_params=pltpu.CompilerParams(
            dimension_semantics=("arbitrary",), vmem_limit_bytes=VMEM_LIMIT_BYTES),
        name="qkv_proj",
    )(x2d, g, w_qkv, col_scale)


def _sb_kernel(q_ref, k_ref, v_ref, u_ref, o_ref, acc_ref, carry_ref, z_ref, hilo_ref, arg_ref, *,
               t, w):
    n_sub = q_ref.shape[1] // t

    tiles = []
    for sub in range(n_sub):
        qi = pl.program_id(2) * n_sub + sub
        q2 = _stack_halves(q_ref[0, sub * t:(sub + 1) * t, :])
        o_tile = o_ref.at[0, sub * t:(sub + 1) * t, :]
        tiles.append((qi, q2, o_tile, _sb_band(qi, q2, k_ref, v_ref, u_ref, o_tile, w=w)))

    for qi, q2, o_tile, done in tiles:
        @pl.when(jnp.logical_not(done))
        def _(qi=qi, q2=q2, o_tile=o_tile):
            _sb_walk(qi, q2, k_ref, v_ref, u_ref, o_tile, acc_ref, carry_ref, z_ref, hilo_ref, arg_ref,
                     t=t, w=w)


def _softplus2(z):
    return jnp.maximum(z, jnp.log(1.0 + jnp.exp2(jnp.minimum(z, SOFTPLUS_CLAMP))) * LOG2E)


def _sb_band(qi, q2, k_ref, v_ref, u_ref, o_ref, *, w):
    t = 2 * w

    def block(ref, kb):
        return ref[0, pl.ds(pl.multiple_of(kb * w, w), w), :]

    b0 = 2 * qi
    bm = jnp.maximum(b0 - 1, 0)
    q_a = jnp.concatenate([q2[0:w], q2[t:t + w]], axis=0)
    q_b = jnp.concatenate([q2[w:t], q2[t + w:2 * t]], axis=0)
    z_0 = lax.dot_general(jnp.concatenate([q_a, q_b], axis=0), block(k_ref, b0), _NT,
                          preferred_element_type=F32)
    z_b_diag = lax.dot_general(q_b, block(k_ref, b0 + 1), _NT, preferred_element_type=F32)
    z_a_left = lax.dot_general(q_a, block(k_ref, bm), _NT, preferred_element_type=F32)
    causal = (lax.broadcasted_iota(jnp.int32, (t, w), 1)
              < (lax.broadcasted_iota(jnp.int32, (t, w), 0) & (w - 1)))
    z_first = jnp.concatenate([jnp.where(causal, z_0[:t], NEG_BIG),
                               jnp.where(causal, z_b_diag, NEG_BIG)], axis=0)
    z_second = jnp.concatenate([jnp.where(qi > 0, z_a_left, NEG_BIG), z_0[t:]], axis=0)

    def log_weights(z):
        sp = _softplus2(z)
        sp16 = sp.astype(BF16)
        right = jnp.dot(sp16, u_ref[2 * w:], preferred_element_type=F32)
        total = right[:, 0:1] + sp16[:, 0:1].astype(F32)
        return (z - sp) - right, jnp.broadcast_to(total, (2 * t, LANES))

    arg_first, tot_first = log_weights(z_first)
    arg_second, tot_second = log_weights(z_second)
    carry = tot_first + tot_second
    a_first = jnp.exp2(arg_first).astype(BF16)
    a_second = jnp.exp2(arg_second - jnp.concatenate([tot_first] * (w // LANES), axis=1)).astype(BF16)
    v_0 = block(v_ref, b0)
    acc_a = (jnp.dot(a_first[:t], v_0, preferred_element_type=F32)
             + jnp.dot(a_second[:t], block(v_ref, bm), preferred_element_type=F32))
    acc_b = (jnp.dot(a_first[t:], block(v_ref, b0 + 1), preferred_element_type=F32)
             + jnp.dot(a_second[t:], v_0, preferred_element_type=F32))
    lane = lax.broadcasted_iota(jnp.int32, (w, LANES), 1)
    o_ref[...] = jnp.concatenate([jnp.where(lane < HEAD_DIM, acc_a[:w], acc_a[w:]),
                                  jnp.where(lane < HEAD_DIM, acc_b[:w], acc_b[w:])], axis=0)
    return jnp.logical_or(qi == 0, jnp.min(carry) >= ZERO_WEIGHT_CARRY)


def _sb_walk(qi, q2, k_ref, v_ref, u_ref, o_ref, acc_ref, carry_ref, z_ref, hilo_ref, arg_ref, *, t, w):
    rows = 2 * t
    top = (qi + 1) * (t // w) - 1
    row = lax.broadcasted_iota(jnp.int32, (rows, w), 0) & (t - 1)
    col = lax.broadcasted_iota(jnp.int32, (rows, w), 1)
    acc_ref[...] = jnp.zeros_like(acc_ref)
    carry_ref[...] = jnp.zeros_like(carry_ref)

    def scores(kb, masked, dst):
        kt = k_ref[0, pl.ds(pl.multiple_of(kb * w, w), w), :]
        z = lax.dot_general(q2, kt, _NT, preferred_element_type=F32)
        if masked:
            z = jnp.where(col + (kb * w - qi * t) < row, z, NEG_BIG)
        sp = _softplus2(z)
        hi = sp.astype(BF16)
        lo = (sp - hi.astype(F32)).astype(BF16)
        hilo_ref[dst] = jnp.concatenate([hi, lo], axis=1)
        z_ref[dst] = z

    def sums(src, dst):
        cum = jnp.dot(hilo_ref[src], u_ref[:2 * w], preferred_element_type=F32)
        c = carry_ref[...]
        arg_ref[dst] = z_ref[src] - cum - jnp.concatenate([c] * (w // LANES), axis=1)
        c = c + jnp.broadcast_to(cum[:, 0:1], (rows, LANES))
        carry_ref[...] = c
        return jnp.min(c)

    def values(kb, src):
        vt = v_ref[0, pl.ds(pl.multiple_of(kb * w, w), w), :]
        acc_ref[...] += jnp.dot(jnp.exp2(arg_ref[src]).astype(BF16), vt, preferred_element_type=F32)

    scores(top, True, 0)
    min_carry = sums(0, 0)
    scores(top - 1, True, 1)

    def live(state):
        p, min_carry = state
        return jnp.logical_and(p < (top - 1) // 2, min_carry < ZERO_WEIGHT_CARRY)

    def pair(state):
        p, _ = state
        kb = top - 2 * p
        scores(kb - 2, False, 0)
        values(kb, 0)
        sums(1, 1)
        scores(kb - 3, False, 1)
        values(kb - 1, 1)
        return p + 1, sums(0, 0)

    p, min_carry = lax.while_loop(live, pair, (jnp.int32(0), min_carry))
    kb = top - 2 * p
    values(kb, 0)

    @pl.when(min_carry < ZERO_WEIGHT_CARRY)
    def _():
        sums(1, 1)
        values(kb - 1, 1)

    lane = lax.broadcasted_iota(jnp.int32, (t, LANES), 1)
    o_ref[...] = jnp.where(lane < HEAD_DIM, acc_ref[:t], acc_ref[t:])


def _sb_attn(qkv, u2, t, w, tiles_per_step):
    b, s, _ = qkv.shape
    rows = 2 * t
    tq = tiles_per_step * t
    return pl.pallas_call(
        functools.partial(_sb_kernel, t=t, w=w),
        grid=(b, SB_HEADS // 2, s // tq),
        in_specs=[
            pl.BlockSpec((1, tq, LANES), lambda bi, p, qi: (bi, qi, SB_Q_BLK + p)),
            pl.BlockSpec((1, s, LANES), lambda bi, p, qi: (bi, 0, SB_K_BLK + p)),
            pl.BlockSpec((1, s, LANES), lambda bi, p, qi: (bi, 0, SB_V_BLK + p)),
            pl.BlockSpec((3 * w, w), lambda bi, p, qi: (0, 0)),
        ],
        out_specs=pl.BlockSpec((1, tq, LANES), lambda bi, p, qi: (bi, qi, p)),
        out_shape=jax.ShapeDtypeStruct((b, s, SB_WIDTH), F32),
        scratch_shapes=[
            pltpu.VMEM((rows, LANES), F32),
            pltpu.VMEM((rows, LANES), F32),
            pltpu.VMEM((2, rows, w), F32),
            pltpu.VMEM((2, rows, 2 * w), BF16),
            pltpu.VMEM((2, rows, w), F32),
        ],
        compiler_params=pltpu.CompilerParams(
            dimension_semantics=("arbitrary", "arbitrary", "arbitrary"),
            vmem_limit_bytes=VMEM_LIMIT_BYTES),
        name="sb_attn",
    )(qkv, qkv, qkv, u2)


def _diff_kernel(q_ref, k_ref, v_ref, lam_ref, slope_ref, g_ref, o_ref, acc_ref, m_ref, alpha_ref,
                 mtile_ref, s_ref, knorm_ref, *, t, out_scale):
    step = pl.program_id(2)
    n_sub = q_ref.shape[1] // t
    slope2 = slope_ref[pl.program_id(1)]
    lane = lax.broadcasted_iota(jnp.int32, (t, LANES), 1)
    lo_half = lane < HEAD_DIM

    @pl.when(step == 0)
    def _():
        def tile_norms(j, best):
            kt = k_ref[0, pl.ds(pl.multiple_of(j * t, t), t), :].astype(F32)
            sq = kt * kt
            n1 = jnp.max(jnp.sum(jnp.where(lo_half, sq, 0.0), axis=-1, keepdims=True))
            n2 = jnp.max(jnp.sum(jnp.where(lo_half, 0.0, sq), axis=-1, keepdims=True))
            best = (jnp.maximum(best[0], n1), jnp.maximum(best[1], n2))
            knorm_ref[0, j] = best[0]
            knorm_ref[1, j] = best[1]
            return best

        lax.fori_loop(0, k_ref.shape[1] // t, tile_norms, (jnp.zeros((), F32), jnp.zeros((), F32)))

    causal = ((lax.broadcasted_iota(jnp.int32, (2 * t, t), 0) & (t - 1))
              >= lax.broadcasted_iota(jnp.int32, (2 * t, t), 1))
    tiles = [_diff_tile(step * n_sub + sub, q_ref[0, sub * t:(sub + 1) * t, :], causal, slope2, k_ref, v_ref,
                        lam_ref, g_ref, o_ref.at[0, sub * t:(sub + 1) * t, :], acc_ref.at[sub],
                        m_ref.at[sub], alpha_ref.at[sub], mtile_ref.at[sub], s_ref.at[sub], knorm_ref,
                        t=t, out_scale=out_scale) for sub in range(n_sub)]
    for tile in tiles:
        next(tile)
    for tile in tiles:
        next(tile, None)


def _diff_tile(qi, q_tile, causal, slope2, k_ref, v_ref, lam_ref, g_ref, o_ref, acc_ref, m_ref, alpha_ref,
               mtile_ref, s_ref, knorm_ref, *, t, out_scale):
    rows = 2 * t
    reps = t // LANES
    q2 = _stack_halves(q_tile)
    key_iota = lax.broadcasted_iota(jnp.int32, (1, t), 1)
    ones = jnp.ones((t, LANES), BF16)

    acc_ref[...] = jnp.zeros_like(acc_ref)
    m_ref[...] = jnp.full_like(m_ref, NEG_BIG)

    def qk(kb):
        kt = k_ref[0, pl.ds(pl.multiple_of(kb * t, t), t), :]
        return lax.dot_general(q2, kt, _NT, preferred_element_type=F32)

    def scores(kb, valid, dst):
        bias = (key_iota + (kb - qi) * t).astype(F32) * slope2
        s2 = qk(kb) + bias
        if valid is not None:
            s2 = jnp.where(valid, s2, NEG_BIG)
        m_old = m_ref[...]
        m_new = jnp.maximum(m_old, jnp.max(s2, axis=-1, keepdims=True))
        s_ref[dst] = s2
        mtile_ref[dst] = m_new
        alpha_ref[dst] = jnp.exp2(m_old - m_new)
        m_ref[...] = m_new

    def min_running_max():
        m = m_ref[...]
        return jnp.min(m[:t]), jnp.min(m[t:])

    def values(kb, src):
        vt = v_ref[0, pl.ds(pl.multiple_of(kb * t, t), t), :]
        alpha = alpha_ref[src]
        p = jnp.exp2(s_ref[src] - jnp.concatenate([mtile_ref[src]] * reps, axis=1))
        acc_ref[...] = (acc_ref[...] * jnp.concatenate([alpha, alpha], axis=1)
                        + jnp.dot(p.astype(BF16), jnp.concatenate([vt, ones], axis=1),
                                  preferred_element_type=F32))

    q_sq = q2.astype(F32)
    q_sq = jnp.sum(q_sq * q_sq, axis=-1, keepdims=True)
    qn1 = jnp.max(q_sq[:t])
    qn2 = jnp.max(q_sq[t:])

    def all_zero(kb, mmin1, mmin2):
        bias_max = ((kb - qi) * t + (t - 1)).astype(F32) * slope2

        def below(qn, kn, mmin):
            room = mmin - ZERO_WEIGHT_MARGIN - bias_max
            return jnp.logical_and(room > 0.0, qn * kn <= room * room)

        return jnp.logical_and(below(qn1, knorm_ref[0, kb], mmin1), below(qn2, knorm_ref[1, kb], mmin2))

    scores(qi, causal, 0)
    first_mmin = min_running_max()
    yield

    def needed(j, mmin):
        kb = jnp.maximum(qi - j, 0)
        return jnp.logical_and(j <= qi, jnp.logical_not(all_zero(kb, *mmin)))

    def walk(n_pairs):
        def body(state):
            j = state[0]
            mmin = min_running_max()
            for i in range(n_pairs):
                scores(qi - j - 2 * i - 1, None, 1)
                values(qi - j - 2 * i, 0)
                scores(qi - j - 2 * i - 2, None, 0)
                values(qi - j - 2 * i - 1, 1)
            return (j + 2 * n_pairs,) + mmin

        return body

    state = (jnp.int32(0),) + first_mmin
    for n_pairs in (2, 1):
        state = lax.while_loop(lambda st, n=2 * n_pairs: needed(st[0] + n, st[1:]), walk(n_pairs), state)
    j = state[0]
    one_more = needed(j + 1, min_running_max())

    @pl.when(one_more)
    def _():
        scores(qi - j - 1, None, 1)
        values(qi - j, 0)
        values(qi - j - 1, 1)

    @pl.when(jnp.logical_not(one_more))
    def _():
        values(qi - j, 0)

    a1 = acc_ref[:t]
    a2 = acc_ref[t:]
    o = a1[:, :LANES] / a1[:, LANES:] - lam_ref[0, 0] * (a2[:, :LANES] / a2[:, LANES:])
    o_ref[...] = _rms_norm(o, g_ref[...]) * out_scale


def _diff_attn(qkv, lam, subln_g, t, out_scale, tiles_per_step):
    b, s, _ = qkv.shape
    rows = 2 * t
    n = tiles_per_step
    tq = n * t
    slopes2 = jnp.asarray([2.0 ** (-8.0 * (h + 1) / DIFF_HEADS) * LOG2E for h in range(DIFF_HEADS)], F32)
    return pl.pallas_call(
        functools.partial(_diff_kernel, t=t, out_scale=out_scale),
        grid=(b, DIFF_HEADS, s // tq),
        in_specs=[
            pl.BlockSpec((1, tq, LANES), lambda bi, h, qi: (bi, qi, DF_Q_BLK + h)),
            pl.BlockSpec((1, s, LANES), lambda bi, h, qi: (bi, 0, DF_K_BLK + h)),
            pl.BlockSpec((1, s, LANES), lambda bi, h, qi: (bi, 0, DF_V_BLK + h)),
            pl.BlockSpec(memory_space=pltpu.SMEM),
            pl.BlockSpec(memory_space=pltpu.SMEM),
            pl.BlockSpec((1, LANES), lambda bi, h, qi: (0, 0)),
        ],
        out_specs=pl.BlockSpec((1, tq, LANES), lambda bi, h, qi: (bi, qi, h)),
        out_shape=jax.ShapeDtypeStruct((b, s, DIFF_WIDTH), F32),
        scratch_shapes=[
            pltpu.VMEM((n, rows, 2 * LANES), F32),
            pltpu.VMEM((n, rows, LANES), F32),
            pltpu.VMEM((n, 2, rows, LANES), F32),
            pltpu.VMEM((n, 2, rows, LANES), F32),
            pltpu.VMEM((n, 2, rows, t), F32),
            pltpu.SMEM((2, s // t), F32),
        ],
        compiler_params=pltpu.CompilerParams(
            dimension_semantics=("arbitrary", "arbitrary", "arbitrary"),
            vmem_limit_bytes=VMEM_LIMIT_BYTES),
        name="diff_attn",
    )(qkv, qkv, qkv, lam, slopes2, subln_g)


def _lambda_kernel(q1_ref, k1_ref, q2_ref, k2_ref, o_ref, *, lambda_init):
    s1 = jnp.sum(q1_ref[...] * k1_ref[...], axis=-1, keepdims=True)
    s2 = jnp.sum(q2_ref[...] * k2_ref[...], axis=-1, keepdims=True)
    o_ref[...] = jnp.exp(s1) - jnp.exp(s2) + lambda_init


def _lambda(q1, k1, q2, k2, lambda_init):
    return pl.pallas_call(
        functools.partial(_lambda_kernel, lambda_init=lambda_init),
        out_shape=jax.ShapeDtypeStruct((1, 1), F32),
        name="diff_lambda",
    )(q1, k1, q2, k2)


def _out_kernel(x_ref, sb_ref, df_ref, g_pre_ref, w_zg_ref, b_ref, w_sb_ref, w_df_ref, w_out_ref,
                g_post_ref, o_ref):
    x = x_ref[...]
    h = _rms_norm(x, g_pre_ref[...]).astype(BF16)
    zg = jnp.dot(h, w_zg_ref[...], preferred_element_type=F32)
    d = x.shape[-1]
    sb_z = zg[:, :SB_WIDTH]
    df_z = zg[:, SB_WIDTH:SB_WIDTH + DIFF_WIDTH]
    gates = jax.nn.sigmoid(zg[:, SB_WIDTH + DIFF_WIDTH:] + b_ref[...])
    y_sb = jnp.dot((sb_ref[...] * (sb_z * jax.nn.sigmoid(sb_z))).astype(BF16), w_sb_ref[...],
                   preferred_element_type=F32)
    y_df = jnp.dot((df_ref[...] * (df_z * jax.nn.sigmoid(df_z))).astype(BF16), w_df_ref[...],
                   preferred_element_type=F32)
    merged = gates[:, :d] * y_sb + gates[:, d:] * y_df
    out = jnp.dot(merged.astype(BF16), w_out_ref[...], preferred_element_type=F32)
    o_ref[...] = x + _rms_norm(out, g_post_ref[...])


def _out_proj(x2d, sb, df, g_pre, w_zg, b_gate, w_sb, w_df, w_out, g_post, tm):
    n, d = x2d.shape
    zg_cols = w_zg.shape[1]
    full = lambda shape: pl.BlockSpec(shape, lambda i: (0, 0))
    return pl.pallas_call(
        _out_kernel,
        grid=(n // tm,),
        in_specs=[
            pl.BlockSpec((tm, d), lambda i: (i, 0)),
            pl.BlockSpec((tm, SB_WIDTH), lambda i: (i, 0)),
            pl.BlockSpec((tm, DIFF_WIDTH), lambda i: (i, 0)),
            full((1, d)),
            full((d, zg_cols)),
            full((1, 2 * d)),
            full((SB_WIDTH, d)),
            full((DIFF_WIDTH, d)),
            full((d, d)),
            full((1, d)),
        ],
        out_specs=pl.BlockSpec((tm, d), lambda i: (i, 0)),
        out_shape=jax.ShapeDtypeStruct((n, d), F32),
        compiler_params=pltpu.CompilerParams(
            dimension_semantics=("arbitrary",), vmem_limit_bytes=VMEM_LIMIT_BYTES),
        name="out_proj",
    )(x2d, sb, df, g_pre, w_zg, b_gate, w_sb, w_df, w_out, g_post)


def _split_w_in(w):
    sizes = (SB_WIDTH,) * 4 + (DIFF_WIDTH,) * 4 + (w.shape[0],) * 2
    parts, off = [], 0
    for n in sizes:
        parts.append(w[:, off:off + n])
        off += n
    return parts


def kernel(x, pre_norm_g, w_in, b_gate, lambda_q1, lambda_k1, lambda_q2, lambda_k2, subln_g, w_o_sb,
           w_o_diff, w_out, post_norm_g):
    b, s, d = x.shape
    depth = w_in.shape[0]
    t, w = ATTN_T, SUM_W
    assert s % t == 0 and s >= 2 * t and t % w == 0
    tm = min(512, b * s)
    tri_r = lax.broadcasted_iota(jnp.int32, (w, w), 0)
    tri_c = lax.broadcasted_iota(jnp.int32, (w, w), 1)
    u = (tri_r >= tri_c).astype(BF16)
    u2 = jnp.concatenate([u, u, (tri_r > tri_c).astype(BF16)], axis=0)
    q_scale = jnp.full((1, SB_WIDTH), QK_SCALE * LOG2E, F32)
    one = jnp.ones((1, SB_WIDTH), F32)
    col_scale = jnp.concatenate([q_scale, one, one, q_scale, one, one], axis=1)
    for layer in range(depth):
        sb_q, sb_k, sb_v, sb_z, df_q, df_k, df_v, df_z, g_sb, g_df = _split_w_in(w_in[layer])
        w_qkv = jnp.concatenate([sb_q, sb_k, sb_v, df_q, df_k, df_v], axis=1).astype(BF16)
        w_zg = jnp.concatenate([sb_z, df_z, g_sb, g_df], axis=1).astype(BF16)
        lambda_init = 0.8 - 0.6 * math.exp(-0.3 * layer)
        x2d = x.reshape(b * s, d)
        g_pre = pre_norm_g[layer].reshape(1, d)

        qkv = _qkv_proj(x2d, g_pre, w_qkv, col_scale, tm).reshape(b, s, QKV_COLS)
        sb_out = _sb_attn(qkv, u2, t, w, SB_TILES_PER_STEP if s % (SB_TILES_PER_STEP * t) == 0 else 1)
        lam = _lambda(lambda_q1[layer][None], lambda_k1[layer][None], lambda_q2[layer][None],
                      lambda_k2[layer][None], lambda_init)
        df_out = _diff_attn(qkv, lam, subln_g[layer].reshape(1, LANES), t, 1.0 - lambda_init,
                            DIFF_TILES_PER_STEP if s % (DIFF_TILES_PER_STEP * t) == 0 else 1)
        x = _out_proj(x2d, sb_out.reshape(b * s, SB_WIDTH), df_out.reshape(b * s, DIFF_WIDTH), g_pre,
                      w_zg, b_gate[layer].reshape(1, 2 * d), w_o_sb[layer].astype(BF16),
                      w_o_diff[layer].astype(BF16), w_out[layer].astype(BF16),
                      post_norm_g[layer].reshape(1, d), tm).reshape(b, s, d)
    return x
```

```python
import functools
import math

import jax
import jax.numpy as jnp
from jax import lax
from jax.experimental import pallas as pl
from jax.experimental.pallas import tpu as pltpu

F32 = jnp.float32
BF16 = jnp.bfloat16

HEAD_DIM = 64
SB_HEADS = 8
DIFF_HEADS = 4
SB_WIDTH = SB_HEADS * HEAD_DIM
DIFF_WIDTH = DIFF_HEADS * 2 * HEAD_DIM
NORM_EPS = 1e-6
LANES = 128
MXU_DIM = 256
LOG2E = 1.4426950408889634
QK_SCALE = HEAD_DIM ** -0.5
NEG_BIG = -1e30
SOFTPLUS_CLAMP = 100.0
ZERO_WEIGHT_CARRY = 152.0
ZERO_WEIGHT_MARGIN = 156.0
VMEM_LIMIT_BYTES = 56 * 1024 * 1024

SUM_W = MXU_DIM
ATTN_T = 2 * SUM_W
SB_TILES_PER_STEP = 4
DIFF_TILES_PER_STEP = 4

QKV_COLS = 3 * SB_WIDTH + 3 * DIFF_WIDTH
SB_Q_BLK, SB_K_BLK, SB_V_BLK = 0, 4, 8
DF_Q_BLK, DF_K_BLK, DF_V_BLK = 12, 16, 20

_NT = (((1,), (1,)), ((), ()))


def _rms_norm(x32, g32):
    return x32 * lax.rsqrt(jnp.mean(x32 * x32, axis=-1, keepdims=True) + NORM_EPS) * g32


def _stack_halves(q):
    q32 = q.astype(F32)
    lane = lax.broadcasted_iota(jnp.int32, q.shape, 1)
    return jnp.concatenate([jnp.where(lane < HEAD_DIM, q32, 0.0).astype(BF16),
                            jnp.where(lane >= HEAD_DIM, q32, 0.0).astype(BF16)], axis=0)


def _qkv_kernel(x_ref, g_ref, w_ref, cs_ref, o_ref):
    h = _rms_norm(x_ref[...], g_ref[...]).astype(BF16)
    p = jnp.dot(h, w_ref[...], preferred_element_type=F32)
    o_ref[...] = (p * cs_ref[...]).astype(BF16)


def _qkv_proj(x2d, g, w_qkv, col_scale, tm):
    n, d = x2d.shape
    return pl.pallas_call(
        _qkv_kernel,
        grid=(n // tm,),
        in_specs=[
            pl.BlockSpec((tm, d), lambda i: (i, 0)),
            pl.BlockSpec((1, d), lambda i: (0, 0)),
            pl.BlockSpec((d, QKV_COLS), lambda i: (0, 0)),
            pl.BlockSpec((1, QKV_COLS), lambda i: (0, 0)),
        ],
        out_specs=pl.BlockSpec((tm, QKV_COLS), lambda i: (i, 0)),
        out_shape=jax.ShapeDtypeStruct((n, QKV_COLS), BF16),
        compiler_params=pltpu.CompilerParams(
            dimension_semantics=("arbitrary",), vmem_limit_bytes=VMEM_LIMIT_BYTES),
        name="qkv_proj",
    )(x2d, g, w_qkv, col_scale)


def _sb_kernel(q_ref, k_ref, v_ref, u_ref, o_ref, acc_ref, carry_ref, z_ref, hilo_ref, arg_ref, *,
               t, w):
    n_sub = q_ref.shape[1] // t

    tiles = []
    for sub in range(n_sub):
        qi = pl.program_id(2) * n_sub + sub
        q2 = _stack_halves(q_ref[0, sub * t:(sub + 1) * t, :])
        o_tile = o_ref.at[0, sub * t:(sub + 1) * t, :]
        tiles.append((qi, q2, o_tile, _sb_band(qi, q2, k_ref, v_ref, u_ref, o_tile, w=w)))

    for qi, q2, o_tile, done in tiles:
        @pl.when(jnp.logical_not(done))
        def _(qi=qi, q2=q2, o_tile=o_tile):
            _sb_walk(qi, q2, k_ref, v_ref, u_ref, o_tile, acc_ref, carry_ref, z_ref, hilo_ref, arg_ref,
                     t=t, w=w)


def _softplus2(z):
    return jnp.maximum(z, jnp.log(1.0 + jnp.exp2(jnp.minimum(z, SOFTPLUS_CLAMP))) * LOG2E)


def _sb_band(qi, q2, k_ref, v_ref, u_ref, o_ref, *, w):
    t = 2 * w

    def block(ref, kb):
        return ref[0, pl.ds(pl.multiple_of(kb * w, w), w), :]

    b0 = 2 * qi
    bm = jnp.maximum(b0 - 1, 0)
    q_a = jnp.concatenate([q2[0:w], q2[t:t + w]], axis=0)
    q_b = jnp.concatenate([q2[w:t], q2[t + w:2 * t]], axis=0)
    z_0 = lax.dot_general(jnp.concatenate([q_a, q_b], axis=0), block(k_ref, b0), _NT,
                          preferred_element_type=F32)
    z_b_diag = lax.dot_general(q_b, block(k_ref, b0 + 1), _NT, preferred_element_type=F32)
    z_a_left = lax.dot_general(q_a, block(k_ref, bm), _NT, preferred_element_type=F32)
    causal = (lax.broadcasted_iota(jnp.int32, (t, w), 1)
              < (lax.broadcasted_iota(jnp.int32, (t, w), 0) & (w - 1)))
    z_first = jnp.concatenate([jnp.where(causal, z_0[:t], NEG_BIG),
                               jnp.where(causal, z_b_diag, NEG_BIG)], axis=0)
    z_second = jnp.concatenate([jnp.where(qi > 0, z_a_left, NEG_BIG), z_0[t:]], axis=0)

    def log_weights(z):
        sp = _softplus2(z)
        sp16 = sp.astype(BF16)
        right = jnp.dot(sp16, u_ref[2 * w:], preferred_element_type=F32)
        total = right[:, 0:1] + sp16[:, 0:1].astype(F32)
        return (z - sp) - right, jnp.broadcast_to(total, (2 * t, LANES))

    arg_first, tot_first = log_weights(z_first)
    arg_second, tot_second = log_weights(z_second)
    carry = tot_first + tot_second
    a_first = jnp.exp2(arg_first).astype(BF16)
    a_second = jnp.exp2(arg_second - jnp.concatenate([tot_first] * (w // LANES), axis=1)).astype(BF16)
    v_0 = block(v_ref, b0)
    acc_a = (jnp.dot(a_first[:t], v_0, preferred_element_type=F32)
             + jnp.dot(a_second[:t], block(v_ref, bm), preferred_element_type=F32))
    acc_b = (jnp.dot(a_first[t:], block(v_ref, b0 + 1), preferred_element_type=F32)
             + jnp.dot(a_second[t:], v_0, preferred_element_type=F32))
    lane = lax.broadcasted_iota(jnp.int32, (w, LANES), 1)
    o_ref[...] = jnp.concatenate([jnp.where(lane < HEAD_DIM, acc_a[:w], acc_a[w:]),
                                  jnp.where(lane < HEAD_DIM, acc_b[:w], acc_b[w:])], axis=0)
    return jnp.logical_or(qi == 0, jnp.min(carry) >= ZERO_WEIGHT_CARRY)


def _sb_walk(qi, q2, k_ref, v_ref, u_ref, o_ref, acc_ref, carry_ref, z_ref, hilo_ref, arg_ref, *, t, w):
    rows = 2 * t
    top = (qi + 1) * (t // w) - 1
    row = lax.broadcasted_iota(jnp.int32, (rows, w), 0) & (t - 1)
    col = lax.broadcasted_iota(jnp.int32, (rows, w), 1)
    acc_ref[...] = jnp.zeros_like(acc_ref)
    carry_ref[...] = jnp.zeros_like(carry_ref)

    def scores(kb, masked, dst):
        kt = k_ref[0, pl.ds(pl.multiple_of(kb * w, w), w), :]
        z = lax.dot_general(q2, kt, _NT, preferred_element_type=F32)
        if masked:
            z = jnp.where(col + (kb * w - qi * t) < row, z, NEG_BIG)
        sp = _softplus2(z)
        hi = sp.astype(BF16)
        lo = (sp - hi.astype(F32)).astype(BF16)
        hilo_ref[dst] = jnp.concatenate([hi, lo], axis=1)
        z_ref[dst] = z

    def sums(src, dst):
        cum = jnp.dot(hilo_ref[src], u_ref[:2 * w], preferred_element_type=F32)
        c = carry_ref[...]
        arg_ref[dst] = z_ref[src] - cum - jnp.concatenate([c] * (w // LANES), axis=1)
        c = c + jnp.broadcast_to(cum[:, 0:1], (rows, LANES))
        carry_ref[...] = c
        return jnp.min(c)

    def values(kb, src):
        vt = v_ref[0, pl.ds(pl.multiple_of(kb * w, w), w), :]
        acc_ref[...] += jnp.dot(jnp.exp2(arg_ref[src]).astype(BF16), vt, preferred_element_type=F32)

    scores(top, True, 0)
    min_carry = sums(0, 0)
    scores(top - 1, True, 1)

    def live(state):
        p, min_carry = state
        return jnp.logical_and(p < (top - 1) // 2, min_carry < ZERO_WEIGHT_CARRY)

    def pair(state):
        p, _ = state
        kb = top - 2 * p
        scores(kb - 2, False, 0)
        values(kb, 0)
        sums(1, 1)
        scores(kb - 3, False, 1)
        values(kb - 1, 1)
        return p + 1, sums(0, 0)

    p, min_carry = lax.while_loop(live, pair, (jnp.int32(0), min_carry))
    kb = top - 2 * p
    values(kb, 0)

    @pl.when(min_carry < ZERO_WEIGHT_CARRY)
    def _():
        sums(1, 1)
        values(kb - 1, 1)

    lane = lax.broadcasted_iota(jnp.int32, (t, LANES), 1)
    o_ref[...] = jnp.where(lane < HEAD_DIM, acc_ref[:t], acc_ref[t:])


def _sb_attn(qkv, u2, t, w, tiles_per_step):
    b, s, _ = qkv.shape
    rows = 2 * t
    tq = tiles_per_step * t
    return pl.pallas_call(
        functools.partial(_sb_kernel, t=t, w=w),
        grid=(b, SB_HEADS // 2, s // tq),
        in_specs=[
            pl.BlockSpec((1, tq, LANES), lambda bi, p, qi: (bi, qi, SB_Q_BLK + p)),
            pl.BlockSpec((1, s, LANES), lambda bi, p, qi: (bi, 0, SB_K_BLK + p)),
            pl.BlockSpec((1, s, LANES), lambda bi, p, qi: (bi, 0, SB_V_BLK + p)),
            pl.BlockSpec((3 * w, w), lambda bi, p, qi: (0, 0)),
        ],
        out_specs=pl.BlockSpec((1, tq, LANES), lambda bi, p, qi: (bi, qi, p)),
        out_shape=jax.ShapeDtypeStruct((b, s, SB_WIDTH), F32),
        scratch_shapes=[
            pltpu.VMEM((rows, LANES), F32),
            pltpu.VMEM((rows, LANES), F32),
            pltpu.VMEM((2, rows, w), F32),
            pltpu.VMEM((2, rows, 2 * w), BF16),
            pltpu.VMEM((2, rows, w), F32),
        ],
        compiler_params=pltpu.CompilerParams(
            dimension_semantics=("arbitrary", "arbitrary", "arbitrary"),
            vmem_limit_bytes=VMEM_LIMIT_BYTES),
        name="sb_attn",
    )(qkv, qkv, qkv, u2)


def _diff_kernel(q_ref, k_ref, v_ref, lam_ref, slope_ref, g_ref, o_ref, acc_ref, m_ref, alpha_ref,
                 mtile_ref, s_ref, knorm_ref, *, t, out_scale):
    step = pl.program_id(2)
    n_sub = q_ref.shape[1] // t
    slope2 = slope_ref[pl.program_id(1)]
    lane = lax.broadcasted_iota(jnp.int32, (t, LANES), 1)
    lo_half = lane < HEAD_DIM

    @pl.when(step == 0)
    def _():
        def tile_norms(j, best):
            kt = k_ref[0, pl.ds(pl.multiple_of(j * t, t), t), :].astype(F32)
            sq = kt * kt
            n1 = jnp.max(jnp.sum(jnp.where(lo_half, sq, 0.0), axis=-1, keepdims=True))
            n2 = jnp.max(jnp.sum(jnp.where(lo_half, 0.0, sq), axis=-1, keepdims=True))
            best = (jnp.maximum(best[0], n1), jnp.maximum(best[1], n2))
            knorm_ref[0, j] = best[0]
            knorm_ref[1, j] = best[1]
            return best

        lax.fori_loop(0, k_ref.shape[1] // t, tile_norms, (jnp.zeros((), F32), jnp.zeros((), F32)))

    causal = ((lax.broadcasted_iota(jnp.int32, (2 * t, t), 0) & (t - 1))
              >= lax.broadcasted_iota(jnp.int32, (2 * t, t), 1))
    tiles = [_diff_tile(step * n_sub + sub, q_ref[0, sub * t:(sub + 1) * t, :], causal, slope2, k_ref, v_ref,
                        lam_ref, g_ref, o_ref.at[0, sub * t:(sub + 1) * t, :], acc_ref.at[sub],
                        m_ref.at[sub], alpha_ref.at[sub], mtile_ref.at[sub], s_ref.at[sub], knorm_ref,
                        t=t, out_scale=out_scale) for sub in range(n_sub)]
    for tile in tiles:
        next(tile)
    for tile in tiles:
        next(tile, None)


def _diff_tile(qi, q_tile, causal, slope2, k_ref, v_ref, lam_ref, g_ref, o_ref, acc_ref, m_ref, alpha_ref,
               mtile_ref, s_ref, knorm_ref, *, t, out_scale):
    rows = 2 * t
    reps = t // LANES
    q2 = _stack_halves(q_tile)
    key_iota = lax.broadcasted_iota(jnp.int32, (1, t), 1)
    ones = jnp.ones((t, LANES), BF16)

    acc_ref[...] = jnp.zeros_like(acc_ref)
    m_ref[...] = jnp.full_like(m_ref, NEG_BIG)

    def qk(kb):
        kt = k_ref[0, pl.ds(pl.multiple_of(kb * t, t), t), :]
        return lax.dot_general(q2, kt, _NT, preferred_element_type=F32)

    def scores(kb, valid, dst):
        bias = (key_iota + (kb - qi) * t).astype(F32) * slope2
        s2 = qk(kb) + bias
        if valid is not None:
            s2 = jnp.where(valid, s2, NEG_BIG)
        m_old = m_ref[...]
        m_new = jnp.maximum(m_old, jnp.max(s2, axis=-1, keepdims=True))
        s_ref[dst] = s2
        mtile_ref[dst] = m_new
        alpha_ref[dst] = jnp.exp2(m_old - m_new)
        m_ref[...] = m_new

    def min_running_max():
        m = m_ref[...]
        return jnp.min(m[:t]), jnp.min(m[t:])

    def values(kb, src):
        vt = v_ref[0, pl.ds(pl.multiple_of(kb * t, t), t), :]
        alpha = alpha_ref[src]
        p = jnp.exp2(s_ref[src] - jnp.concatenate([mtile_ref[src]] * reps, axis=1))
        acc_ref[...] = (acc_ref[...] * jnp.concatenate([alpha, alpha], axis=1)
                        + jnp.dot(p.astype(BF16), jnp.concatenate([vt, ones], axis=1),
                                  preferred_element_type=F32))

    q_sq = q2.astype(F32)
    q_sq = jnp.sum(q_sq * q_sq, axis=-1, keepdims=True)
    qn1 = jnp.max(q_sq[:t])
    qn2 = jnp.max(q_sq[t:])

    def all_zero(kb, mmin1, mmin2):
        bias_max = ((kb - qi) * t + (t - 1)).astype(F32) * slope2

        def below(qn, kn, mmin):
            room = mmin - ZERO_WEIGHT_MARGIN - bias_max
            return jnp.logical_and(room > 0.0, qn * kn <= room * room)

        return jnp.logical_and(below(qn1, knorm_ref[0, kb], mmin1), below(qn2, knorm_ref[1, kb], mmin2))

    scores(qi, causal, 0)
    first_mmin = min_running_max()
    yield

    def needed(j, mmin):
        kb = jnp.maximum(qi - j, 0)
        return jnp.logical_and(j <= qi, jnp.logical_not(all_zero(kb, *mmin)))

    def walk(n_pairs):
        def body(state):
            j = state[0]
            mmin = min_running_max()
            for i in range(n_pairs):
                scores(qi - j - 2 * i - 1, None, 1)
                values(qi - j - 2 * i, 0)
                scores(qi - j - 2 * i - 2, None, 0)
                values(qi - j - 2 * i - 1, 1)
            return (j + 2 * n_pairs,) + mmin

        return body

    state = (jnp.int32(0),) + first_mmin
    for n_pairs in (2, 1):
        state = lax.while_loop(lambda st, n=2 * n_pairs: needed(st[0] + n, st[1:]), walk(n_pairs), state)
    j = state[0]
    one_more = needed(j + 1, min_running_max())

    @pl.when(one_more)
    def _():
        scores(qi - j - 1, None, 1)
        values(qi - j, 0)
        values(qi - j - 1, 1)

    @pl.when(jnp.logical_not(one_more))
    def _():
        values(qi - j, 0)

    a1 = acc_ref[:t]
    a2 = acc_ref[t:]
    o = a1[:, :LANES] / a1[:, LANES:] - lam_ref[0, 0] * (a2[:, :LANES] / a2[:, LANES:])
    o_ref[...] = _rms_norm(o, g_ref[...]) * out_scale


def _diff_attn(qkv, lam, subln_g, t, out_scale, tiles_per_step):
    b, s, _ = qkv.shape
    rows = 2 * t
    n = tiles_per_step
    tq = n * t
    slopes2 = jnp.asarray([2.0 ** (-8.0 * (h + 1) / DIFF_HEADS) * LOG2E for h in range(DIFF_HEADS)], F32)
    return pl.pallas_call(
        functools.partial(_diff_kernel, t=t, out_scale=out_scale),
        grid=(b, DIFF_HEADS, s // tq),
        in_specs=[
            pl.BlockSpec((1, tq, LANES), lambda bi, h, qi: (bi, qi, DF_Q_BLK + h)),
            pl.BlockSpec((1, s, LANES), lambda bi, h, qi: (bi, 0, DF_K_BLK + h)),
            pl.BlockSpec((1, s, LANES), lambda bi, h, qi: (bi, 0, DF_V_BLK + h)),
            pl.BlockSpec(memory_space=pltpu.SMEM),
            pl.BlockSpec(memory_space=pltpu.SMEM),
            pl.BlockSpec((1, LANES), lambda bi, h, qi: (0, 0)),
        ],
        out_specs=pl.BlockSpec((1, tq, LANES), lambda bi, h, qi: (bi, qi, h)),
        out_shape=jax.ShapeDtypeStruct((b, s, DIFF_WIDTH), F32),
        scratch_shapes=[
            pltpu.VMEM((n, rows, 2 * LANES), F32),
            pltpu.VMEM((n, rows, LANES), F32),
            pltpu.VMEM((n, 2, rows, LANES), F32),
            pltpu.VMEM((n, 2, rows, LANES), F32),
            pltpu.VMEM((n, 2, rows, t), F32),
            pltpu.SMEM((2, s // t), F32),
        ],
        compiler_params=pltpu.CompilerParams(
            dimension_semantics=("arbitrary", "arbitrary", "arbitrary"),
            vmem_limit_bytes=VMEM_LIMIT_BYTES),
        name="diff_attn",
    )(qkv, qkv, qkv, lam, slopes2, subln_g)


def _lambda_kernel(q1_ref, k1_ref, q2_ref, k2_ref, o_ref, *, lambda_init):
    s1 = jnp.sum(q1_ref[...] * k1_ref[...], axis=-1, keepdims=True)
    s2 = jnp.sum(q2_ref[...] * k2_ref[...], axis=-1, keepdims=True)
    o_ref[...] = jnp.exp(s1) - jnp.exp(s2) + lambda_init


def _lambda(q1, k1, q2, k2, lambda_init):
    return pl.pallas_call(
        functools.partial(_lambda_kernel, lambda_init=lambda_init),
        out_shape=jax.ShapeDtypeStruct((1, 1), F32),
        name="diff_lambda",
    )(q1, k1, q2, k2)


def _out_kernel(x_ref, sb_ref, df_ref, g_pre_ref, w_zg_ref, b_ref, w_sb_ref, w_df_ref, w_out_ref,
                g_post_ref, o_ref):
    x = x_ref[...]
    h = _rms_norm(x, g_pre_ref[...]).astype(BF16)
    zg = jnp.dot(h, w_zg_ref[...], preferred_element_type=F32)
    d = x.shape[-1]
    sb_z = zg[:, :SB_WIDTH]
    df_z = zg[:, SB_WIDTH:SB_WIDTH + DIFF_WIDTH]
    gates = jax.nn.sigmoid(zg[:, SB_WIDTH + DIFF_WIDTH:] + b_ref[...])
    y_sb = jnp.dot((sb_ref[...] * (sb_z * jax.nn.sigmoid(sb_z))).astype(BF16), w_sb_ref[...],
                   preferred_element_type=F32)
    y_df = jnp.dot((df_ref[...] * (df_z * jax.nn.sigmoid(df_z))).astype(BF16), w_df_ref[...],
                   preferred_element_type=F32)
    merged = gates[:, :d] * y_sb + gates[:, d:] * y_df
    out = jnp.dot(merged.astype(BF16), w_out_ref[...], preferred_element_type=F32)
    o_ref[...] = x + _rms_norm(out, g_post_ref[...])


def _out_proj(x2d, sb, df, g_pre, w_zg, b_gate, w_sb, w_df, w_out, g_post, tm):
    n, d = x2d.shape
    zg_cols = w_zg.shape[1]
    full = lambda shape: pl.BlockSpec(shape, lambda i: (0, 0))
    return pl.pallas_call(
        _out_kernel,
        grid=(n // tm,),
        in_specs=[
            pl.BlockSpec((tm, d), lambda i: (i, 0)),
            pl.BlockSpec((tm, SB_WIDTH), lambda i: (i, 0)),
            pl.BlockSpec((tm, DIFF_WIDTH), lambda i: (i, 0)),
            full((1, d)),
            full((d, zg_cols)),
            full((1, 2 * d)),
            full((SB_WIDTH, d)),
            full((DIFF_WIDTH, d)),
            full((d, d)),
            full((1, d)),
        ],
        out_specs=pl.BlockSpec((tm, d), lambda i: (i, 0)),
        out_shape=jax.ShapeDtypeStruct((n, d), F32),
        compiler_params=pltpu.CompilerParams(
            dimension_semantics=("arbitrary",), vmem_limit_bytes=VMEM_LIMIT_BYTES),
        name="out_proj",
    )(x2d, sb, df, g_pre, w_zg, b_gate, w_sb, w_df, w_out, g_post)


def _split_w_in(w):
    sizes = (SB_WIDTH,) * 4 + (DIFF_WIDTH,) * 4 + (w.shape[0],) * 2
    parts, off = [], 0
    for n in sizes:
        parts.append(w[:, off:off + n])
        off += n
    return parts


def kernel(x, pre_norm_g, w_in, b_gate, lambda_q1, lambda_k1, lambda_q2, lambda_k2, subln_g, w_o_sb,
           w_o_diff, w_out, post_norm_g):
    b, s, d = x.shape
    depth = w_in.shape[0]
    t, w = ATTN_T, SUM_W
    assert s % t == 0 and s >= 2 * t and t % w == 0
    tm = min(512, b * s)
    tri_r = lax.broadcasted_iota(jnp.int32, (w, w), 0)
    tri_c = lax.broadcasted_iota(jnp.int32, (w, w), 1)
    u = (tri_r >= tri_c).astype(BF16)
    u2 = jnp.concatenate([u, u, (tri_r > tri_c).astype(BF16)], axis=0)
    q_scale = jnp.full((1, SB_WIDTH), QK_SCALE * LOG2E, F32)
    one = jnp.ones((1, SB_WIDTH), F32)
    col_scale = jnp.concatenate([q_scale, one, one, q_scale, one, one], axis=1)
    for layer in range(depth):
        sb_q, sb_k, sb_v, sb_z, df_q, df_k, df_v, df_z, g_sb, g_df = _split_w_in(w_in[layer])
        w_qkv = jnp.concatenate([sb_q, sb_k, sb_v, df_q, df_k, df_v], axis=1).astype(BF16)
        w_zg = jnp.concatenate([sb_z, df_z, g_sb, g_df], axis=1).astype(BF16)
        lambda_init = 0.8 - 0.6 * math.exp(-0.3 * layer)
        x2d = x.reshape(b * s, d)
        g_pre = pre_norm_g[layer].reshape(1, d)

        qkv = _qkv_proj(x2d, g_pre, w_qkv, col_scale, tm).reshape(b, s, QKV_COLS)
        sb_out = _sb_attn(qkv, u2, t, w, SB_TILES_PER_STEP if s % (SB_TILES_PER_STEP * t) == 0 else 1)
        lam = _lambda(lambda_q1[layer][None], lambda_k1[layer][None], lambda_q2[layer][None],
                      lambda_k2[layer][None], lambda_init)
        df_out = _diff_attn(qkv, lam, subln_g[layer].reshape(1, LANES), t, 1.0 - lambda_init,
                            DIFF_TILES_PER_STEP if s % (DIFF_TILES_PER_STEP * t) == 0 else 1)
        x = _out_proj(x2d, sb_out.reshape(b * s, SB_WIDTH), df_out.reshape(b * s, DIFF_WIDTH), g_pre,
                      w_zg, b_gate[layer].reshape(1, 2 * d), w_o_sb[layer].astype(BF16),
                      w_o_diff[layer].astype(BF16), w_out[layer].astype(BF16),
                      post_norm_g[layer].reshape(1, d), tm).reshape(b, s, d)
    return x
```

```python
import functools
import math

import jax
import jax.numpy as jnp
from jax import lax
from jax.experimental import pallas as pl
from jax.experimental.pallas import tpu as pltpu

F32 = jnp.float32
BF16 = jnp.bfloat16

HEAD_DIM = 64
SB_HEADS = 8
DIFF_HEADS = 4
SB_WIDTH = SB_HEADS * HEAD_DIM
DIFF_WIDTH = DIFF_HEADS * 2 * HEAD_DIM
NORM_EPS = 1e-6
LANES = 128
MXU_DIM = 256
LOG2E = 1.4426950408889634
QK_SCALE = HEAD_DIM ** -0.5
NEG_BIG = -1e30
SOFTPLUS_CLAMP = 100.0
ZERO_WEIGHT_CARRY = 152.0
ZERO_WEIGHT_MARGIN = 156.0
VMEM_LIMIT_BYTES = 56 * 1024 * 1024

SUM_W = MXU_DIM
ATTN_T = 2 * SUM_W
SB_TILES_PER_STEP = 4
DIFF_TILES_PER_STEP = 4

QKV_COLS = 3 * SB_WIDTH + 3 * DIFF_WIDTH
SB_Q_BLK, SB_K_BLK, SB_V_BLK = 0, 4, 8
DF_Q_BLK, DF_K_BLK, DF_V_BLK = 12, 16, 20

_NT = (((1,), (1,)), ((), ()))


def _rms_norm(x32, g32):
    return x32 * lax.rsqrt(jnp.mean(x32 * x32, axis=-1, keepdims=True) + NORM_EPS) * g32


def _stack_halves(q):
    q32 = q.astype(F32)
    lane = lax.broadcasted_iota(jnp.int32, q.shape, 1)
    return jnp.concatenate([jnp.where(lane < HEAD_DIM, q32, 0.0).astype(BF16),
                            jnp.where(lane >= HEAD_DIM, q32, 0.0).astype(BF16)], axis=0)


def _qkv_kernel(x_ref, g_ref, w_ref, cs_ref, o_ref, kn_ref):
    h = _rms_norm(x_ref[...], g_ref[...]).astype(BF16)
    p = jnp.dot(h, w_ref[...], preferred_element_type=F32)
    qkv = (p * cs_ref[...]).astype(BF16)
    o_ref[...] = qkv
    lo_half = lax.broadcasted_iota(jnp.int32, (x_ref.shape[0], LANES), 1) < HEAD_DIM
    for hd in range(DIFF_HEADS):
        k32 = qkv[:, (DF_K_BLK + hd) * LANES:(DF_K_BLK + hd + 1) * LANES].astype(F32)
        sq = k32 * k32
        for mp, part in enumerate((jnp.where(lo_half, sq, 0.0), jnp.where(lo_half, 0.0, sq))):
            biggest = jnp.max(jnp.sum(part, axis=-1, keepdims=True), axis=0, keepdims=True)
            kn_ref[0, 2 * hd + mp:2 * hd + mp + 1, :] = jnp.broadcast_to(biggest, (1, LANES))


def _qkv_proj(x2d, g, w_qkv, col_scale, tm):
    n, d = x2d.shape
    return pl.pallas_call(
        _qkv_kernel,
        grid=(n // tm,),
        in_specs=[
            pl.BlockSpec((tm, d), lambda i: (i, 0)),
            pl.BlockSpec((1, d), lambda i: (0, 0)),
            pl.BlockSpec((d, QKV_COLS), lambda i: (0, 0)),
            pl.BlockSpec((1, QKV_COLS), lambda i: (0, 0)),
        ],
        out_specs=[pl.BlockSpec((tm, QKV_COLS), lambda i: (i, 0)),
                   pl.BlockSpec((1, 2 * DIFF_HEADS, LANES), lambda i: (i, 0, 0))],
        out_shape=[jax.ShapeDtypeStruct((n, QKV_COLS), BF16),
                   jax.ShapeDtypeStruct((n // tm, 2 * DIFF_HEADS, LANES), F32)],
        compiler_params=pltpu.CompilerParams(
            dimension_semantics=("arbitrary",), vmem_limit_bytes=VMEM_LIMIT_BYTES),
        name="qkv_proj",
    )(x2d, g, w_qkv, col_scale)


def _sb_kernel(q_ref, k_ref, v_ref, u_ref, o_ref, acc_ref, carry_ref, z_ref, hilo_ref, arg_ref, *,
               t, w):
    n_sub = q_ref.shape[1] // t

    tiles = []
    for sub in range(n_sub):
        qi = pl.program_id(2) * n_sub + sub
        q2 = _stack_halves(q_ref[0, sub * t:(sub + 1) * t, :])
        o_tile = o_ref.at[0, sub * t:(sub + 1) * t, :]
        tiles.append((qi, q2, o_tile, _sb_band(qi, q2, k_ref, v_ref, u_ref, o_tile, w=w)))

    for qi, q2, o_tile, done in tiles:
        @pl.when(jnp.logical_not(done))
        def _(qi=qi, q2=q2, o_tile=o_tile):
            _sb_walk(qi, q2, k_ref, v_ref, u_ref, o_tile, acc_ref, carry_ref, z_ref, hilo_ref, arg_ref,
                     t=t, w=w)


def _softplus2(z):
    return jnp.maximum(z, jnp.log(1.0 + jnp.exp2(jnp.minimum(z, SOFTPLUS_CLAMP))) * LOG2E)


def _sb_band(qi, q2, k_ref, v_ref, u_ref, o_ref, *, w):
    t = 2 * w

    def block(ref, kb):
        return ref[0, pl.ds(pl.multiple_of(kb * w, w), w), :]

    b0 = 2 * qi
    bm = jnp.maximum(b0 - 1, 0)
    q_a = jnp.concatenate([q2[0:w], q2[t:t + w]], axis=0)
    q_b = jnp.concatenate([q2[w:t], q2[t + w:2 * t]], axis=0)
    z_0 = lax.dot_general(jnp.concatenate([q_a, q_b], axis=0), block(k_ref, b0), _NT,
                          preferred_element_type=F32)
    z_b_diag = lax.dot_general(q_b, block(k_ref, b0 + 1), _NT, preferred_element_type=F32)
    z_a_left = lax.dot_general(q_a, block(k_ref, bm), _NT, preferred_element_type=F32)
    causal = (lax.broadcasted_iota(jnp.int32, (t, w), 1)
              < (lax.broadcasted_iota(jnp.int32, (t, w), 0) & (w - 1)))
    z_first = jnp.concatenate([jnp.where(causal, z_0[:t], NEG_BIG),
                               jnp.where(causal, z_b_diag, NEG_BIG)], axis=0)
    z_second = jnp.concatenate([jnp.where(qi > 0, z_a_left, NEG_BIG), z_0[t:]], axis=0)

    def log_weights(z):
        sp = _softplus2(z)
        sp16 = sp.astype(BF16)
        right = jnp.dot(sp16, u_ref[2 * w:], preferred_element_type=F32)
        total = right[:, 0:1] + sp16[:, 0:1].astype(F32)
        return (z - sp) - right, jnp.broadcast_to(total, (2 * t, LANES))

    arg_first, tot_first = log_weights(z_first)
    arg_second, tot_second = log_weights(z_second)
    carry = tot_first + tot_second
    a_first = jnp.exp2(arg_first).astype(BF16)
    a_second = jnp.exp2(arg_second - jnp.concatenate([tot_first] * (w // LANES), axis=1)).astype(BF16)
    v_0 = block(v_ref, b0)
    acc_a = (jnp.dot(a_first[:t], v_0, preferred_element_type=F32)
             + jnp.dot(a_second[:t], block(v_ref, bm), preferred_element_type=F32))
    acc_b = (jnp.dot(a_first[t:], block(v_ref, b0 + 1), preferred_element_type=F32)
             + jnp.dot(a_second[t:], v_0, preferred_element_type=F32))
    lane = lax.broadcasted_iota(jnp.int32, (w, LANES), 1)
    o_ref[...] = jnp.concatenate([jnp.where(lane < HEAD_DIM, acc_a[:w], acc_a[w:]),
                                  jnp.where(lane < HEAD_DIM, acc_b[:w], acc_b[w:])], axis=0)
    return jnp.logical_or(qi == 0, jnp.min(carry) >= ZERO_WEIGHT_CARRY)


def _sb_walk(qi, q2, k_ref, v_ref, u_ref, o_ref, acc_ref, carry_ref, z_ref, hilo_ref, arg_ref, *, t, w):
    rows = 2 * t
    top = (qi + 1) * (t // w) - 1
    row = lax.broadcasted_iota(jnp.int32, (rows, w), 0) & (t - 1)
    col = lax.broadcasted_iota(jnp.int32, (rows, w), 1)
    acc_ref[...] = jnp.zeros_like(acc_ref)
    carry_ref[...] = jnp.zeros_like(carry_ref)

    def scores(kb, masked, dst):
        kt = k_ref[0, pl.ds(pl.multiple_of(kb * w, w), w), :]
        z = lax.dot_general(q2, kt, _NT, preferred_element_type=F32)
        if masked:
            z = jnp.where(col + (kb * w - qi * t) < row, z, NEG_BIG)
        sp = _softplus2(z)
        hi = sp.astype(BF16)
        lo = (sp - hi.astype(F32)).astype(BF16)
        hilo_ref[dst] = jnp.concatenate([hi, lo], axis=1)
        z_ref[dst] = z

    def sums(src, dst):
        cum = jnp.dot(hilo_ref[src], u_ref[:2 * w], preferred_element_type=F32)
        c = carry_ref[...]
        arg_ref[dst] = z_ref[src] - cum - jnp.concatenate([c] * (w // LANES), axis=1)
        c = c + jnp.broadcast_to(cum[:, 0:1], (rows, LANES))
        carry_ref[...] = c
        return jnp.min(c)

    def values(kb, src):
        vt = v_ref[0, pl.ds(pl.multiple_of(kb * w, w), w), :]
        acc_ref[...] += jnp.dot(jnp.exp2(arg_ref[src]).astype(BF16), vt, preferred_element_type=F32)

    scores(top, True, 0)
    min_carry = sums(0, 0)
    scores(top - 1, True, 1)

    def live(state):
        p, min_carry = state
        return jnp.logical_and(p < (top - 1) // 2, min_carry < ZERO_WEIGHT_CARRY)

    def pair(state):
        p, _ = state
        kb = top - 2 * p
        scores(kb - 2, False, 0)
        values(kb, 0)
        sums(1, 1)
        scores(kb - 3, False, 1)
        values(kb - 1, 1)
        return p + 1, sums(0, 0)

    p, min_carry = lax.while_loop(live, pair, (jnp.int32(0), min_carry))
    kb = top - 2 * p
    values(kb, 0)

    @pl.when(min_carry < ZERO_WEIGHT_CARRY)
    def _():
        sums(1, 1)
        values(kb - 1, 1)

    lane = lax.broadcasted_iota(jnp.int32, (t, LANES), 1)
    o_ref[...] = jnp.where(lane < HEAD_DIM, acc_ref[:t], acc_ref[t:])


def _sb_attn(qkv, u2, t, w, tiles_per_step):
    b, s, _ = qkv.shape
    rows = 2 * t
    tq = tiles_per_step * t
    return pl.pallas_call(
        functools.partial(_sb_kernel, t=t, w=w),
        grid=(b, SB_HEADS // 2, s // tq),
        in_specs=[
            pl.BlockSpec((1, tq, LANES), lambda bi, p, qi: (bi, qi, SB_Q_BLK + p)),
            pl.BlockSpec((1, s, LANES), lambda bi, p, qi: (bi, 0, SB_K_BLK + p)),
            pl.BlockSpec((1, s, LANES), lambda bi, p, qi: (bi, 0, SB_V_BLK + p)),
            pl.BlockSpec((3 * w, w), lambda bi, p, qi: (0, 0)),
        ],
        out_specs=pl.BlockSpec((1, tq, LANES), lambda bi, p, qi: (bi, qi, p)),
        out_shape=jax.ShapeDtypeStruct((b, s, SB_WIDTH), F32),
        scratch_shapes=[
            pltpu.VMEM((rows, LANES), F32),
            pltpu.VMEM((rows, LANES), F32),
            pltpu.VMEM((2, rows, w), F32),
            pltpu.VMEM((2, rows, 2 * w), BF16),
            pltpu.VMEM((2, rows, w), F32),
        ],
        compiler_params=pltpu.CompilerParams(
            dimension_semantics=("arbitrary", "arbitrary", "arbitrary"),
            vmem_limit_bytes=VMEM_LIMIT_BYTES),
        name="sb_attn",
    )(qkv, qkv, qkv, u2)


def _diff_kernel(q_ref, k_ref, v_ref, lam_ref, slope_ref, kn_ref, g_ref, o_ref, acc_ref, m_ref, alpha_ref,
                 mtile_ref, s_ref, knorm_ref, *, t, out_scale):
    bi = pl.program_id(0)
    hd = pl.program_id(1)
    step = pl.program_id(2)
    n_sub = q_ref.shape[1] // t
    n_tiles = k_ref.shape[1] // t
    slope2 = slope_ref[hd]

    @pl.when(step == 0)
    def _():
        def tile_norms(j, best):
            at = j * (2 * DIFF_HEADS) + 2 * hd
            best = (jnp.maximum(best[0], kn_ref[bi, at]), jnp.maximum(best[1], kn_ref[bi, at + 1]))
            knorm_ref[0, j] = best[0]
            knorm_ref[1, j] = best[1]
            return best

        lax.fori_loop(0, n_tiles, tile_norms, (jnp.zeros((), F32), jnp.zeros((), F32)))

    causal = ((lax.broadcasted_iota(jnp.int32, (2 * t, t), 0) & (t - 1))
              >= lax.broadcasted_iota(jnp.int32, (2 * t, t), 1))
    tiles = [_diff_tile(step * n_sub + sub, q_ref[0, sub * t:(sub + 1) * t, :], causal, slope2, k_ref, v_ref,
                        lam_ref, g_ref, o_ref.at[0, sub * t:(sub + 1) * t, :], acc_ref.at[sub],
                        m_ref.at[sub], alpha_ref.at[sub], mtile_ref.at[sub], s_ref.at[sub], knorm_ref,
                        t=t, out_scale=out_scale) for sub in range(n_sub)]
    for tile in tiles:
        next(tile)
    for tile in tiles:
        next(tile, None)


def _diff_tile(qi, q_tile, causal, slope2, k_ref, v_ref, lam_ref, g_ref, o_ref, acc_ref, m_ref, alpha_ref,
               mtile_ref, s_ref, knorm_ref, *, t, out_scale):
    rows = 2 * t
    reps = t // LANES
    q2 = _stack_halves(q_tile)
    key_iota = lax.broadcasted_iota(jnp.int32, (1, t), 1)
    ones = jnp.ones((t, LANES), BF16)

    acc_ref[...] = jnp.zeros_like(acc_ref)
    m_ref[...] = jnp.full_like(m_ref, NEG_BIG)

    def qk(kb):
        kt = k_ref[0, pl.ds(pl.multiple_of(kb * t, t), t), :]
        return lax.dot_general(q2, kt, _NT, preferred_element_type=F32)

    def scores(kb, valid, dst):
        bias = (key_iota + (kb - qi) * t).astype(F32) * slope2
        s2 = qk(kb) + bias
        if valid is not None:
            s2 = jnp.where(valid, s2, NEG_BIG)
        m_old = m_ref[...]
        m_new = jnp.maximum(m_old, jnp.max(s2, axis=-1, keepdims=True))
        s_ref[dst] = s2
        mtile_ref[dst] = m_new
        alpha_ref[dst] = jnp.exp2(m_old - m_new)
        m_ref[...] = m_new

    def min_running_max():
        m = m_ref[...]
        return jnp.min(m[:t]), jnp.min(m[t:])

    def values(kb, src):
        vt = v_ref[0, pl.ds(pl.multiple_of(kb * t, t), t), :]
        alpha = alpha_ref[src]
        p = jnp.exp2(s_ref[src] - jnp.concatenate([mtile_ref[src]] * reps, axis=1))
        acc_ref[...] = (acc_ref[...] * jnp.concatenate([alpha, alpha], axis=1)
                        + jnp.dot(p.astype(BF16), jnp.concatenate([vt, ones], axis=1),
                                  preferred_element_type=F32))

    q_sq = q2.astype(F32)
    q_sq = jnp.sum(q_sq * q_sq, axis=-1, keepdims=True)
    qn1 = jnp.max(q_sq[:t])
    qn2 = jnp.max(q_sq[t:])

    def all_zero(kb, mmin1, mmin2):
        bias_max = ((kb - qi) * t + (t - 1)).astype(F32) * slope2

        def below(qn, kn, mmin):
            room = mmin - ZERO_WEIGHT_MARGIN - bias_max
            return jnp.logical_and(room > 0.0, qn * kn <= room * room)

        return jnp.logical_and(below(qn1, knorm_ref[0, kb], mmin1), below(qn2, knorm_ref[1, kb], mmin2))

    scores(qi, causal, 0)
    first_mmin = min_running_max()
    yield

    def needed(j, mmin):
        kb = jnp.maximum(qi - j, 0)
        return jnp.logical_and(j <= qi, jnp.logical_not(all_zero(kb, *mmin)))

    def walk(n_pairs):
        def body(state):
            j = state[0]
            mmin = min_running_max()
            for i in range(n_pairs):
                scores(qi - j - 2 * i - 1, None, 1)
                values(qi - j - 2 * i, 0)
                scores(qi - j - 2 * i - 2, None, 0)
                values(qi - j - 2 * i - 1, 1)
            return (j + 2 * n_pairs,) + mmin

        return body

    state = (jnp.int32(0),) + first_mmin
    for n_pairs in (2, 1):
        state = lax.while_loop(lambda st, n=2 * n_pairs: needed(st[0] + n, st[1:]), walk(n_pairs), state)
    j = state[0]
    one_more = needed(j + 1, min_running_max())

    @pl.when(one_more)
    def _():
        scores(qi - j - 1, None, 1)
        values(qi - j, 0)
        values(qi - j - 1, 1)

    @pl.when(jnp.logical_not(one_more))
    def _():
        values(qi - j, 0)

    a1 = acc_ref[:t]
    a2 = acc_ref[t:]
    o = a1[:, :LANES] / a1[:, LANES:] - lam_ref[0, 0] * (a2[:, :LANES] / a2[:, LANES:])
    o_ref[...] = _rms_norm(o, g_ref[...]) * out_scale


def _diff_attn(qkv, key_norms, lam, subln_g, t, out_scale, tiles_per_step):
    b, s, _ = qkv.shape
    rows = 2 * t
    n = tiles_per_step
    tq = n * t
    slopes2 = jnp.asarray([2.0 ** (-8.0 * (h + 1) / DIFF_HEADS) * LOG2E for h in range(DIFF_HEADS)], F32)
    return pl.pallas_call(
        functools.partial(_diff_kernel, t=t, out_scale=out_scale),
        grid=(b, DIFF_HEADS, s // tq),
        in_specs=[
            pl.BlockSpec((1, tq, LANES), lambda bi, h, qi: (bi, qi, DF_Q_BLK + h)),
            pl.BlockSpec((1, s, LANES), lambda bi, h, qi: (bi, 0, DF_K_BLK + h)),
            pl.BlockSpec((1, s, LANES), lambda bi, h, qi: (bi, 0, DF_V_BLK + h)),
            pl.BlockSpec(memory_space=pltpu.SMEM),
            pl.BlockSpec(memory_space=pltpu.SMEM),
            pl.BlockSpec(memory_space=pltpu.SMEM),
            pl.BlockSpec((1, LANES), lambda bi, h, qi: (0, 0)),
        ],
        out_specs=pl.BlockSpec((1, tq, LANES), lambda bi, h, qi: (bi, qi, h)),
        out_shape=jax.ShapeDtypeStruct((b, s, DIFF_WIDTH), F32),
        scratch_shapes=[
            pltpu.VMEM((n, rows, 2 * LANES), F32),
            pltpu.VMEM((n, rows, LANES), F32),
            pltpu.VMEM((n, 2, rows, LANES), F32),
            pltpu.VMEM((n, 2, rows, LANES), F32),
            pltpu.VMEM((n, 2, rows, t), F32),
            pltpu.SMEM((2, s // t), F32),
        ],
        compiler_params=pltpu.CompilerParams(
            dimension_semantics=("arbitrary", "arbitrary", "arbitrary"),
            vmem_limit_bytes=VMEM_LIMIT_BYTES),
        name="diff_attn",
    )(qkv, qkv, qkv, lam, slopes2, key_norms, subln_g)


def _lambda_kernel(q1_ref, k1_ref, q2_ref, k2_ref, o_ref, *, lambda_init):
    s1 = jnp.sum(q1_ref[...] * k1_ref[...], axis=-1, keepdims=True)
    s2 = jnp.sum(q2_ref[...] * k2_ref[...], axis=-1, keepdims=True)
    o_ref[...] = jnp.exp(s1) - jnp.exp(s2) + lambda_init


def _lambda(q1, k1, q2, k2, lambda_init):
    return pl.pallas_call(
        functools.partial(_lambda_kernel, lambda_init=lambda_init),
        out_shape=jax.ShapeDtypeStruct((1, 1), F32),
        name="diff_lambda",
    )(q1, k1, q2, k2)


def _out_kernel(x_ref, sb_ref, df_ref, g_pre_ref, w_zg_ref, b_ref, w_sb_ref, w_df_ref, w_out_ref,
                g_post_ref, o_ref):
    x = x_ref[...]
    h = _rms_norm(x, g_pre_ref[...]).astype(BF16)
    zg = jnp.dot(h, w_zg_ref[...], preferred_element_type=F32)
    d = x.shape[-1]
    sb_z = zg[:, :SB_WIDTH]
    df_z = zg[:, SB_WIDTH:SB_WIDTH + DIFF_WIDTH]
    gates = jax.nn.sigmoid(zg[:, SB_WIDTH + DIFF_WIDTH:] + b_ref[...])
    y_sb = jnp.dot((sb_ref[...] * (sb_z * jax.nn.sigmoid(sb_z))).astype(BF16), w_sb_ref[...],
                   preferred_element_type=F32)
    y_df = jnp.dot((df_ref[...] * (df_z * jax.nn.sigmoid(df_z))).astype(BF16), w_df_ref[...],
                   preferred_element_type=F32)
    merged = gates[:, :d] * y_sb + gates[:, d:] * y_df
    out = jnp.dot(merged.astype(BF16), w_out_ref[...], preferred_element_type=F32)
    o_ref[...] = x + _rms_norm(out, g_post_ref[...])


def _out_proj(x2d, sb, df, g_pre, w_zg, b_gate, w_sb, w_df, w_out, g_post, tm):
    n, d = x2d.shape
    zg_cols = w_zg.shape[1]
    full = lambda shape: pl.BlockSpec(shape, lambda i: (0, 0))
    return pl.pallas_call(
        _out_kernel,
        grid=(n // tm,),
        in_specs=[
            pl.BlockSpec((tm, d), lambda i: (i, 0)),
            pl.BlockSpec((tm, SB_WIDTH), lambda i: (i, 0)),
            pl.BlockSpec((tm, DIFF_WIDTH), lambda i: (i, 0)),
            full((1, d)),
            full((d, zg_cols)),
            full((1, 2 * d)),
            full((SB_WIDTH, d)),
            full((DIFF_WIDTH, d)),
            full((d, d)),
            full((1, d)),
        ],
        out_specs=pl.BlockSpec((tm, d), lambda i: (i, 0)),
        out_shape=jax.ShapeDtypeStruct((n, d), F32),
        compiler_params=pltpu.CompilerParams(
            dimension_semantics=("arbitrary",), vmem_limit_bytes=VMEM_LIMIT_BYTES),
        name="out_proj",
    )(x2d, sb, df, g_pre, w_zg, b_gate, w_sb, w_df, w_out, g_post)


def _split_w_in(w):
    sizes = (SB_WIDTH,) * 4 + (DIFF_WIDTH,) * 4 + (w.shape[0],) * 2
    parts, off = [], 0
    for n in sizes:
        parts.append(w[:, off:off + n])
        off += n
    return parts


def kernel(x, pre_norm_g, w_in, b_gate, lambda_q1, lambda_k1, lambda_q2, lambda_k2, subln_g, w_o_sb,
           w_o_diff, w_out, post_norm_g):
    b, s, d = x.shape
    depth = w_in.shape[0]
    t, w = ATTN_T, SUM_W
    assert s % t == 0 and s >= 2 * t and t % w == 0
    tm = t
    tri_r = lax.broadcasted_iota(jnp.int32, (w, w), 0)
    tri_c = lax.broadcasted_iota(jnp.int32, (w, w), 1)
    u = (tri_r >= tri_c).astype(BF16)
    u2 = jnp.concatenate([u, u, (tri_r > tri_c).astype(BF16)], axis=0)
    q_scale = jnp.full((1, SB_WIDTH), QK_SCALE * LOG2E, F32)
    one = jnp.ones((1, SB_WIDTH), F32)
    col_scale = jnp.concatenate([q_scale, one, one, q_scale, one, one], axis=1)
    for layer in range(depth):
        sb_q, sb_k, sb_v, sb_z, df_q, df_k, df_v, df_z, g_sb, g_df = _split_w_in(w_in[layer])
        w_qkv = jnp.concatenate([sb_q, sb_k, sb_v, df_q, df_k, df_v], axis=1).astype(BF16)
        w_zg = jnp.concatenate([sb_z, df_z, g_sb, g_df], axis=1).astype(BF16)
        lambda_init = 0.8 - 0.6 * math.exp(-0.3 * layer)
        x2d = x.reshape(b * s, d)
        g_pre = pre_norm_g[layer].reshape(1, d)

        qkv, key_norms = _qkv_proj(x2d, g_pre, w_qkv, col_scale, tm)
        qkv = qkv.reshape(b, s, QKV_COLS)
        key_norms = key_norms[:, :, 0].reshape(b, (s // t) * 2 * DIFF_HEADS)
        sb_out = _sb_attn(qkv, u2, t, w, SB_TILES_PER_STEP if s % (SB_TILES_PER_STEP * t) == 0 else 1)
        lam = _lambda(lambda_q1[layer][None], lambda_k1[layer][None], lambda_q2[layer][None],
                      lambda_k2[layer][None], lambda_init)
        df_out = _diff_attn(qkv, key_norms, lam, subln_g[layer].reshape(1, LANES), t, 1.0 - lambda_init,
                            DIFF_TILES_PER_STEP if s % (DIFF_TILES_PER_STEP * t) == 0 else 1)
        x = _out_proj(x2d, sb_out.reshape(b * s, SB_WIDTH), df_out.reshape(b * s, DIFF_WIDTH), g_pre,
                      w_zg, b_gate[layer].reshape(1, 2 * d), w_o_sb[layer].astype(BF16),
                      w_o_diff[layer].astype(BF16), w_out[layer].astype(BF16),
                      post_norm_g[layer].reshape(1, d), tm).reshape(b, s, d)
    return x
```

```python
import functools
import math

import jax
import jax.numpy as jnp
from jax import lax
from jax.experimental import pallas as pl
from jax.experimental.pallas import tpu as pltpu

F32 = jnp.float32
BF16 = jnp.bfloat16

HEAD_DIM = 64
SB_HEADS = 8
DIFF_HEADS = 4
SB_WIDTH = SB_HEADS * HEAD_DIM
DIFF_WIDTH = DIFF_HEADS * 2 * HEAD_DIM
NORM_EPS = 1e-6
LANES = 128
MXU_DIM = 256
LOG2E = 1.4426950408889634
QK_SCALE = HEAD_DIM ** -0.5
NEG_BIG = -1e30
SOFTPLUS_CLAMP = 100.0
ZERO_WEIGHT_CARRY = 152.0
ZERO_WEIGHT_MARGIN = 156.0
VMEM_LIMIT_BYTES = 56 * 1024 * 1024

NORM_ROWS = 4 * DIFF_HEADS
SUM_W = MXU_DIM
ATTN_T = 2 * SUM_W
SB_TILES_PER_STEP = 4
DIFF_TILES_PER_STEP = 4

QKV_COLS = 3 * SB_WIDTH + 3 * DIFF_WIDTH
SB_Q_BLK, SB_K_BLK, SB_V_BLK = 0, 4, 8
DF_Q_BLK, DF_K_BLK, DF_V_BLK = 12, 16, 20

_NT = (((1,), (1,)), ((), ()))


def _rms_norm(x32, g32):
    return x32 * lax.rsqrt(jnp.mean(x32 * x32, axis=-1, keepdims=True) + NORM_EPS) * g32


def _stack_halves(q):
    q32 = q.astype(F32)
    lane = lax.broadcasted_iota(jnp.int32, q.shape, 1)
    return jnp.concatenate([jnp.where(lane < HEAD_DIM, q32, 0.0).astype(BF16),
                            jnp.where(lane >= HEAD_DIM, q32, 0.0).astype(BF16)], axis=0)


def _qkv_kernel(x_ref, g_ref, w_ref, cs_ref, o_ref, kn_ref):
    h = _rms_norm(x_ref[...], g_ref[...]).astype(BF16)
    p = jnp.dot(h, w_ref[...], preferred_element_type=F32)
    qkv = (p * cs_ref[...]).astype(BF16)
    o_ref[...] = qkv
    lo_half = lax.broadcasted_iota(jnp.int32, (x_ref.shape[0], LANES), 1) < HEAD_DIM
    for base, first_blk in ((0, DF_K_BLK), (NORM_ROWS // 2, DF_Q_BLK)):
        for hd in range(DIFF_HEADS):
            v32 = qkv[:, (first_blk + hd) * LANES:(first_blk + hd + 1) * LANES].astype(F32)
            sq = v32 * v32
            for mp, part in enumerate((jnp.where(lo_half, sq, 0.0), jnp.where(lo_half, 0.0, sq))):
                biggest = jnp.max(jnp.sum(part, axis=-1, keepdims=True), axis=0, keepdims=True)
                r = base + 2 * hd + mp
                kn_ref[0, r:r + 1, :] = jnp.broadcast_to(biggest, (1, LANES))


def _qkv_proj(x2d, g, w_qkv, col_scale, tm):
    n, d = x2d.shape
    return pl.pallas_call(
        _qkv_kernel,
        grid=(n // tm,),
        in_specs=[
            pl.BlockSpec((tm, d), lambda i: (i, 0)),
            pl.BlockSpec((1, d), lambda i: (0, 0)),
            pl.BlockSpec((d, QKV_COLS), lambda i: (0, 0)),
            pl.BlockSpec((1, QKV_COLS), lambda i: (0, 0)),
        ],
        out_specs=[pl.BlockSpec((tm, QKV_COLS), lambda i: (i, 0)),
                   pl.BlockSpec((1, NORM_ROWS, LANES), lambda i: (i, 0, 0))],
        out_shape=[jax.ShapeDtypeStruct((n, QKV_COLS), BF16),
                   jax.ShapeDtypeStruct((n // tm, NORM_ROWS, LANES), F32)],
        compiler_params=pltpu.CompilerParams(
            dimension_semantics=("arbitrary",), vmem_limit_bytes=VMEM_LIMIT_BYTES),
        name="qkv_proj",
    )(x2d, g, w_qkv, col_scale)


def _sb_kernel(q_ref, k_ref, v_ref, u_ref, o_ref, acc_ref, carry_ref, z_ref, hilo_ref, arg_ref, *,
               t, w):
    n_sub = q_ref.shape[1] // t

    tiles = []
    for sub in range(n_sub):
        qi = pl.program_id(2) * n_sub + sub
        q2 = _stack_halves(q_ref[0, sub * t:(sub + 1) * t, :])
        o_tile = o_ref.at[0, sub * t:(sub + 1) * t, :]
        tiles.append((qi, q2, o_tile, _sb_band(qi, q2, k_ref, v_ref, u_ref, o_tile, w=w)))

    for qi, q2, o_tile, done in tiles:
        @pl.when(jnp.logical_not(done))
        def _(qi=qi, q2=q2, o_tile=o_tile):
            _sb_walk(qi, q2, k_ref, v_ref, u_ref, o_tile, acc_ref, carry_ref, z_ref, hilo_ref, arg_ref,
                     t=t, w=w)


def _softplus2(z):
    return jnp.maximum(z, jnp.log(1.0 + jnp.exp2(jnp.minimum(z, SOFTPLUS_CLAMP))) * LOG2E)


def _sb_band(qi, q2, k_ref, v_ref, u_ref, o_ref, *, w):
    t = 2 * w

    def block(ref, kb):
        return ref[0, pl.ds(pl.multiple_of(kb * w, w), w), :]

    b0 = 2 * qi
    bm = jnp.maximum(b0 - 1, 0)
    q_a = jnp.concatenate([q2[0:w], q2[t:t + w]], axis=0)
    q_b = jnp.concatenate([q2[w:t], q2[t + w:2 * t]], axis=0)
    z_0 = lax.dot_general(jnp.concatenate([q_a, q_b], axis=0), block(k_ref, b0), _NT,
                          preferred_element_type=F32)
    z_b_diag = lax.dot_general(q_b, block(k_ref, b0 + 1), _NT, preferred_element_type=F32)
    z_a_left = lax.dot_general(q_a, block(k_ref, bm), _NT, preferred_element_type=F32)
    causal = (lax.broadcasted_iota(jnp.int32, (t, w), 1)
              < (lax.broadcasted_iota(jnp.int32, (t, w), 0) & (w - 1)))
    z_first = jnp.concatenate([jnp.where(causal, z_0[:t], NEG_BIG),
                               jnp.where(causal, z_b_diag, NEG_BIG)], axis=0)
    z_second = jnp.concatenate([jnp.where(qi > 0, z_a_left, NEG_BIG), z_0[t:]], axis=0)

    def log_weights(z):
        sp = _softplus2(z)
        sp16 = sp.astype(BF16)
        right = jnp.dot(sp16, u_ref[2 * w:], preferred_element_type=F32)
        total = right[:, 0:1] + sp16[:, 0:1].astype(F32)
        return (z - sp) - right, jnp.broadcast_to(total, (2 * t, LANES))

    arg_first, tot_first = log_weights(z_first)
    arg_second, tot_second = log_weights(z_second)
    carry = tot_first + tot_second
    a_first = jnp.exp2(arg_first).astype(BF16)
    a_second = jnp.exp2(arg_second - jnp.concatenate([tot_first] * (w // LANES), axis=1)).astype(BF16)
    v_0 = block(v_ref, b0)
    acc_a = (jnp.dot(a_first[:t], v_0, preferred_element_type=F32)
             + jnp.dot(a_second[:t], block(v_ref, bm), preferred_element_type=F32))
    acc_b = (jnp.dot(a_first[t:], block(v_ref, b0 + 1), preferred_element_type=F32)
             + jnp.dot(a_second[t:], v_0, preferred_element_type=F32))
    lane = lax.broadcasted_iota(jnp.int32, (w, LANES), 1)
    o_ref[...] = jnp.concatenate([jnp.where(lane < HEAD_DIM, acc_a[:w], acc_a[w:]),
                                  jnp.where(lane < HEAD_DIM, acc_b[:w], acc_b[w:])], axis=0)
    return jnp.logical_or(qi == 0, jnp.min(carry) >= ZERO_WEIGHT_CARRY)


def _sb_walk(qi, q2, k_ref, v_ref, u_ref, o_ref, acc_ref, carry_ref, z_ref, hilo_ref, arg_ref, *, t, w):
    rows = 2 * t
    top = (qi + 1) * (t // w) - 1
    row = lax.broadcasted_iota(jnp.int32, (rows, w), 0) & (t - 1)
    col = lax.broadcasted_iota(jnp.int32, (rows, w), 1)
    acc_ref[...] = jnp.zeros_like(acc_ref)
    carry_ref[...] = jnp.zeros_like(carry_ref)

    def scores(kb, masked, dst):
        kt = k_ref[0, pl.ds(pl.multiple_of(kb * w, w), w), :]
        z = lax.dot_general(q2, kt, _NT, preferred_element_type=F32)
        if masked:
            z = jnp.where(col + (kb * w - qi * t) < row, z, NEG_BIG)
        sp = _softplus2(z)
        hi = sp.astype(BF16)
        lo = (sp - hi.astype(F32)).astype(BF16)
        hilo_ref[dst] = jnp.concatenate([hi, lo], axis=1)
        z_ref[dst] = z

    def sums(src, dst):
        cum = jnp.dot(hilo_ref[src], u_ref[:2 * w], preferred_element_type=F32)
        c = carry_ref[...]
        arg_ref[dst] = z_ref[src] - cum - jnp.concatenate([c] * (w // LANES), axis=1)
        c = c + jnp.broadcast_to(cum[:, 0:1], (rows, LANES))
        carry_ref[...] = c
        return jnp.min(c)

    def values(kb, src):
        vt = v_ref[0, pl.ds(pl.multiple_of(kb * w, w), w), :]
        acc_ref[...] += jnp.dot(jnp.exp2(arg_ref[src]).astype(BF16), vt, preferred_element_type=F32)

    scores(top, True, 0)
    min_carry = sums(0, 0)
    scores(top - 1, True, 1)

    def live(state):
        p, min_carry = state
        return jnp.logical_and(p < (top - 1) // 2, min_carry < ZERO_WEIGHT_CARRY)

    def pair(state):
        p, _ = state
        kb = top - 2 * p
        scores(kb - 2, False, 0)
        values(kb, 0)
        sums(1, 1)
        scores(kb - 3, False, 1)
        values(kb - 1, 1)
        return p + 1, sums(0, 0)

    p, min_carry = lax.while_loop(live, pair, (jnp.int32(0), min_carry))
    kb = top - 2 * p
    values(kb, 0)

    @pl.when(min_carry < ZERO_WEIGHT_CARRY)
    def _():
        sums(1, 1)
        values(kb - 1, 1)

    lane = lax.broadcasted_iota(jnp.int32, (t, LANES), 1)
    o_ref[...] = jnp.where(lane < HEAD_DIM, acc_ref[:t], acc_ref[t:])


def _sb_attn(qkv, u2, t, w, tiles_per_step):
    b, s, _ = qkv.shape
    rows = 2 * t
    tq = tiles_per_step * t
    return pl.pallas_call(
        functools.partial(_sb_kernel, t=t, w=w),
        grid=(b, SB_HEADS // 2, s // tq),
        in_specs=[
            pl.BlockSpec((1, tq, LANES), lambda bi, p, qi: (bi, qi, SB_Q_BLK + p)),
            pl.BlockSpec((1, s, LANES), lambda bi, p, qi: (bi, 0, SB_K_BLK + p)),
            pl.BlockSpec((1, s, LANES), lambda bi, p, qi: (bi, 0, SB_V_BLK + p)),
            pl.BlockSpec((3 * w, w), lambda bi, p, qi: (0, 0)),
        ],
        out_specs=pl.BlockSpec((1, tq, LANES), lambda bi, p, qi: (bi, qi, p)),
        out_shape=jax.ShapeDtypeStruct((b, s, SB_WIDTH), F32),
        scratch_shapes=[
            pltpu.VMEM((rows, LANES), F32),
            pltpu.VMEM((rows, LANES), F32),
            pltpu.VMEM((2, rows, w), F32),
            pltpu.VMEM((2, rows, 2 * w), BF16),
            pltpu.VMEM((2, rows, w), F32),
        ],
        compiler_params=pltpu.CompilerParams(
            dimension_semantics=("arbitrary", "arbitrary", "arbitrary"),
            vmem_limit_bytes=VMEM_LIMIT_BYTES),
        name="sb_attn",
    )(qkv, qkv, qkv, u2)


def _diff_kernel(q_ref, k_ref, v_ref, lam_ref, slope_ref, kn_ref, g_ref, o_ref, acc_ref, m_ref, alpha_ref,
                 mtile_ref, s_ref, knorm_ref, *, t, out_scale):
    bi = pl.program_id(0)
    hd = pl.program_id(1)
    step = pl.program_id(2)
    n_sub = q_ref.shape[1] // t
    n_tiles = k_ref.shape[1] // t
    slope2 = slope_ref[hd]

    @pl.when(step == 0)
    def _():
        def tile_norms(j, best):
            at = j * NORM_ROWS + 2 * hd
            best = (jnp.maximum(best[0], kn_ref[bi, at]), jnp.maximum(best[1], kn_ref[bi, at + 1]))
            knorm_ref[0, j] = best[0]
            knorm_ref[1, j] = best[1]
            return best

        lax.fori_loop(0, n_tiles, tile_norms, (jnp.zeros((), F32), jnp.zeros((), F32)))

    causal = ((lax.broadcasted_iota(jnp.int32, (2 * t, t), 0) & (t - 1))
              >= lax.broadcasted_iota(jnp.int32, (2 * t, t), 1))
    def q_norms(qi):
        at = qi * NORM_ROWS + NORM_ROWS // 2 + 2 * hd
        return kn_ref[bi, at], kn_ref[bi, at + 1]

    tiles = [_diff_tile(step * n_sub + sub, q_ref[0, sub * t:(sub + 1) * t, :], q_norms(step * n_sub + sub),
                        causal, slope2, k_ref, v_ref,
                        lam_ref, g_ref, o_ref.at[0, sub * t:(sub + 1) * t, :], acc_ref.at[sub],
                        m_ref.at[sub], alpha_ref.at[sub], mtile_ref.at[sub], s_ref.at[sub], knorm_ref,
                        t=t, out_scale=out_scale) for sub in range(n_sub)]
    for tile in tiles:
        next(tile)
    for tile in tiles:
        next(tile, None)


def _diff_tile(qi, q_tile, q_norms, causal, slope2, k_ref, v_ref, lam_ref, g_ref, o_ref, acc_ref, m_ref,
               alpha_ref, mtile_ref, s_ref, knorm_ref, *, t, out_scale):
    rows = 2 * t
    reps = t // LANES
    q2 = _stack_halves(q_tile)
    key_iota = lax.broadcasted_iota(jnp.int32, (1, t), 1)
    ones = jnp.ones((t, LANES), BF16)

    acc_ref[...] = jnp.zeros_like(acc_ref)
    m_ref[...] = jnp.full_like(m_ref, NEG_BIG)

    def qk(kb):
        kt = k_ref[0, pl.ds(pl.multiple_of(kb * t, t), t), :]
        return lax.dot_general(q2, kt, _NT, preferred_element_type=F32)

    def scores(kb, valid, dst):
        bias = (key_iota + (kb - qi) * t).astype(F32) * slope2
        s2 = qk(kb) + bias
        if valid is not None:
            s2 = jnp.where(valid, s2, NEG_BIG)
        m_old = m_ref[...]
        m_new = jnp.maximum(m_old, jnp.max(s2, axis=-1, keepdims=True))
        s_ref[dst] = s2
        mtile_ref[dst] = m_new
        alpha_ref[dst] = jnp.exp2(m_old - m_new)
        m_ref[...] = m_new

    def min_running_max():
        m = m_ref[...]
        return jnp.min(m[:t]), jnp.min(m[t:])

    def values(kb, src):
        vt = v_ref[0, pl.ds(pl.multiple_of(kb * t, t), t), :]
        alpha = alpha_ref[src]
        p = jnp.exp2(s_ref[src] - jnp.concatenate([mtile_ref[src]] * reps, axis=1))
        acc_ref[...] = (acc_ref[...] * jnp.concatenate([alpha, alpha], axis=1)
                        + jnp.dot(p.astype(BF16), jnp.concatenate([vt, ones], axis=1),
                                  preferred_element_type=F32))

    qn1, qn2 = q_norms

    def all_zero(kb, mmin1, mmin2):
        bias_max = ((kb - qi) * t + (t - 1)).astype(F32) * slope2

        def below(qn, kn, mmin):
            room = mmin - ZERO_WEIGHT_MARGIN - bias_max
            return jnp.logical_and(room > 0.0, qn * kn <= room * room)

        return jnp.logical_and(below(qn1, knorm_ref[0, kb], mmin1), below(qn2, knorm_ref[1, kb], mmin2))

    scores(qi, causal, 0)
    first_mmin = min_running_max()
    yield

    def needed(j, mmin):
        kb = jnp.maximum(qi - j, 0)
        return jnp.logical_and(j <= qi, jnp.logical_not(all_zero(kb, *mmin)))

    def walk(n_pairs):
        def body(state):
            j = state[0]
            mmin = min_running_max()
            for i in range(n_pairs):
                scores(qi - j - 2 * i - 1, None, 1)
                values(qi - j - 2 * i, 0)
                scores(qi - j - 2 * i - 2, None, 0)
                values(qi - j - 2 * i - 1, 1)
            return (j + 2 * n_pairs,) + mmin

        return body

    state = (jnp.int32(0),) + first_mmin
    for n_pairs in (2, 1):
        state = lax.while_loop(lambda st, n=2 * n_pairs: needed(st[0] + n, st[1:]), walk(n_pairs), state)
    j = state[0]
    one_more = needed(j + 1, min_running_max())

    @pl.when(one_more)
    def _():
        scores(qi - j - 1, None, 1)
        values(qi - j, 0)
        values(qi - j - 1, 1)

    @pl.when(jnp.logical_not(one_more))
    def _():
        values(qi - j, 0)

    a1 = acc_ref[:t]
    a2 = acc_ref[t:]
    o = a1[:, :LANES] / a1[:, LANES:] - lam_ref[0, 0] * (a2[:, :LANES] / a2[:, LANES:])
    o_ref[...] = _rms_norm(o, g_ref[...]) * out_scale


def _diff_attn(qkv, key_norms, lam, subln_g, t, out_scale, tiles_per_step):
    b, s, _ = qkv.shape
    rows = 2 * t
    n = tiles_per_step
    tq = n * t
    slopes2 = jnp.asarray([2.0 ** (-8.0 * (h + 1) / DIFF_HEADS) * LOG2E for h in range(DIFF_HEADS)], F32)
    return pl.pallas_call(
        functools.partial(_diff_kernel, t=t, out_scale=out_scale),
        grid=(b, DIFF_HEADS, s // tq),
        in_specs=[
            pl.BlockSpec((1, tq, LANES), lambda bi, h, qi: (bi, qi, DF_Q_BLK + h)),
            pl.BlockSpec((1, s, LANES), lambda bi, h, qi: (bi, 0, DF_K_BLK + h)),
            pl.BlockSpec((1, s, LANES), lambda bi, h, qi: (bi, 0, DF_V_BLK + h)),
            pl.BlockSpec(memory_space=pltpu.SMEM),
            pl.BlockSpec(memory_space=pltpu.SMEM),
            pl.BlockSpec(memory_space=pltpu.SMEM),
            pl.BlockSpec((1, LANES), lambda bi, h, qi: (0, 0)),
        ],
        out_specs=pl.BlockSpec((1, tq, LANES), lambda bi, h, qi: (bi, qi, h)),
        out_shape=jax.ShapeDtypeStruct((b, s, DIFF_WIDTH), F32),
        scratch_shapes=[
            pltpu.VMEM((n, rows, 2 * LANES), F32),
            pltpu.VMEM((n, rows, LANES), F32),
            pltpu.VMEM((n, 2, rows, LANES), F32),
            pltpu.VMEM((n, 2, rows, LANES), F32),
            pltpu.VMEM((n, 2, rows, t), F32),
            pltpu.SMEM((2, s // t), F32),
        ],
        compiler_params=pltpu.CompilerParams(
            dimension_semantics=("arbitrary", "arbitrary", "arbitrary"),
            vmem_limit_bytes=VMEM_LIMIT_BYTES),
        name="diff_attn",
    )(qkv, qkv, qkv, lam, slopes2, key_norms, subln_g)


def _lambda_kernel(q1_ref, k1_ref, q2_ref, k2_ref, o_ref, *, lambda_init):
    s1 = jnp.sum(q1_ref[...] * k1_ref[...], axis=-1, keepdims=True)
    s2 = jnp.sum(q2_ref[...] * k2_ref[...], axis=-1, keepdims=True)
    o_ref[...] = jnp.exp(s1) - jnp.exp(s2) + lambda_init


def _lambda(q1, k1, q2, k2, lambda_init):
    return pl.pallas_call(
        functools.partial(_lambda_kernel, lambda_init=lambda_init),
        out_shape=jax.ShapeDtypeStruct((1, 1), F32),
        name="diff_lambda",
    )(q1, k1, q2, k2)


def _out_kernel(x_ref, sb_ref, df_ref, g_pre_ref, w_zg_ref, b_ref, w_sb_ref, w_df_ref, w_out_ref,
                g_post_ref, o_ref):
    x = x_ref[...]
    h = _rms_norm(x, g_pre_ref[...]).astype(BF16)
    zg = jnp.dot(h, w_zg_ref[...], preferred_element_type=F32)
    d = x.shape[-1]
    sb_z = zg[:, :SB_WIDTH]
    df_z = zg[:, SB_WIDTH:SB_WIDTH + DIFF_WIDTH]
    gates = jax.nn.sigmoid(zg[:, SB_WIDTH + DIFF_WIDTH:] + b_ref[...])
    y_sb = jnp.dot((sb_ref[...] * (sb_z * jax.nn.sigmoid(sb_z))).astype(BF16), w_sb_ref[...],
                   preferred_element_type=F32)
    y_df = jnp.dot((df_ref[...] * (df_z * jax.nn.sigmoid(df_z))).astype(BF16), w_df_ref[...],
                   preferred_element_type=F32)
    merged = gates[:, :d] * y_sb + gates[:, d:] * y_df
    out = jnp.dot(merged.astype(BF16), w_out_ref[...], preferred_element_type=F32)
    o_ref[...] = x + _rms_norm(out, g_post_ref[...])


def _out_proj(x2d, sb, df, g_pre, w_zg, b_gate, w_sb, w_df, w_out, g_post, tm):
    n, d = x2d.shape
    zg_cols = w_zg.shape[1]
    full = lambda shape: pl.BlockSpec(shape, lambda i: (0, 0))
    return pl.pallas_call(
        _out_kernel,
        grid=(n // tm,),
        in_specs=[
            pl.BlockSpec((tm, d), lambda i: (i, 0)),
            pl.BlockSpec((tm, SB_WIDTH), lambda i: (i, 0)),
            pl.BlockSpec((tm, DIFF_WIDTH), lambda i: (i, 0)),
            full((1, d)),
            full((d, zg_cols)),
            full((1, 2 * d)),
            full((SB_WIDTH, d)),
            full((DIFF_WIDTH, d)),
            full((d, d)),
            full((1, d)),
        ],
        out_specs=pl.BlockSpec((tm, d), lambda i: (i, 0)),
        out_shape=jax.ShapeDtypeStruct((n, d), F32),
        compiler_params=pltpu.CompilerParams(
            dimension_semantics=("arbitrary",), vmem_limit_bytes=VMEM_LIMIT_BYTES),
        name="out_proj",
    )(x2d, sb, df, g_pre, w_zg, b_gate, w_sb, w_df, w_out, g_post)


def _split_w_in(w):
    sizes = (SB_WIDTH,) * 4 + (DIFF_WIDTH,) * 4 + (w.shape[0],) * 2
    parts, off = [], 0
    for n in sizes:
        parts.append(w[:, off:off + n])
        off += n
    return parts


def kernel(x, pre_norm_g, w_in, b_gate, lambda_q1, lambda_k1, lambda_q2, lambda_k2, subln_g, w_o_sb,
           w_o_diff, w_out, post_norm_g):
    b, s, d = x.shape
    depth = w_in.shape[0]
    t, w = ATTN_T, SUM_W
    assert s % t == 0 and s >= 2 * t and t % w == 0
    tm = t
    tri_r = lax.broadcasted_iota(jnp.int32, (w, w), 0)
    tri_c = lax.broadcasted_iota(jnp.int32, (w, w), 1)
    u = (tri_r >= tri_c).astype(BF16)
    u2 = jnp.concatenate([u, u, (tri_r > tri_c).astype(BF16)], axis=0)
    q_scale = jnp.full((1, SB_WIDTH), QK_SCALE * LOG2E, F32)
    one = jnp.ones((1, SB_WIDTH), F32)
    col_scale = jnp.concatenate([q_scale, one, one, q_scale, one, one], axis=1)
    for layer in range(depth):
        sb_q, sb_k, sb_v, sb_z, df_q, df_k, df_v, df_z, g_sb, g_df = _split_w_in(w_in[layer])
        w_qkv = jnp.concatenate([sb_q, sb_k, sb_v, df_q, df_k, df_v], axis=1).astype(BF16)
        w_zg = jnp.concatenate([sb_z, df_z, g_sb, g_df], axis=1).astype(BF16)
        lambda_init = 0.8 - 0.6 * math.exp(-0.3 * layer)
        x2d = x.reshape(b * s, d)
        g_pre = pre_norm_g[layer].reshape(1, d)

        qkv, key_norms = _qkv_proj(x2d, g_pre, w_qkv, col_scale, tm)
        qkv = qkv.reshape(b, s, QKV_COLS)
        key_norms = key_norms[:, :, 0].reshape(b, (s // t) * NORM_ROWS)
        sb_out = _sb_attn(qkv, u2, t, w, SB_TILES_PER_STEP if s % (SB_TILES_PER_STEP * t) == 0 else 1)
        lam = _lambda(lambda_q1[layer][None], lambda_k1[layer][None], lambda_q2[layer][None],
                      lambda_k2[layer][None], lambda_init)
        df_out = _diff_attn(qkv, key_norms, lam, subln_g[layer].reshape(1, LANES), t, 1.0 - lambda_init,
                            DIFF_TILES_PER_STEP if s % (DIFF_TILES_PER_STEP * t) == 0 else 1)
        x = _out_proj(x2d, sb_out.reshape(b * s, SB_WIDTH), df_out.reshape(b * s, DIFF_WIDTH), g_pre,
                      w_zg, b_gate[layer].reshape(1, 2 * d), w_o_sb[layer].astype(BF16),
                      w_o_diff[layer].astype(BF16), w_out[layer].astype(BF16),
                      post_norm_g[layer].reshape(1, d), tm).reshape(b, s, d)
    return x
```
